```python
import jax, jax.numpy as jnp
from jax import lax
import numpy as np

D_MODEL = 1024
BATCH = 32
SEQ = 256
DEPTH = 1
DEC_BATCH = 4
DEC_SEQ = 4096
PAST_LEN = 512

GRID_W = 64
N_RET_HEADS = 4
RET_DK = 128
RET_DV = 128
RET_QK_WIDTH = N_RET_HEADS * RET_DK
RET_WIDTH = N_RET_HEADS * RET_DV
POOL_WINDOWS = (2, 4, 8, 16)
N_POOL_GROUPS = 4
POOL_GROUP = 128
POOL_WIDTH = N_POOL_GROUPS * POOL_GROUP
MIX_WIDTH = RET_WIDTH + POOL_WIDTH
IN_WIDTH = 2 * RET_QK_WIDTH + 2 * RET_WIDTH + POOL_WIDTH
D_FF = 2816
CHUNK = 128
ROPE_BASE = 10000.0
N_MOD = 9
EPS = 1e-6

kernel_name = "hybrid_retention_pool_macaron_dit_step"


def _rmsnorm(x, g):
    xf = x.astype(jnp.float32)
    y = xf * lax.rsqrt(jnp.mean(xf * xf, axis=-1, keepdims=True) + EPS) * g.astype(jnp.float32)
    return y.astype(x.dtype)


def _modulate(h, shift, scale):
    return h * (1.0 + scale[:, None, :]) + shift[:, None, :]


def _swiglu(h, w1, w3, w2):
    return (jax.nn.silu(h @ w1) * (h @ w3)) @ w2


def _rotary_2d(x):
    L = x.shape[1]
    rows = L // GRID_W
    row = jnp.repeat(jnp.arange(rows, dtype=jnp.float32), GRID_W)
    col = jnp.tile(jnp.arange(GRID_W, dtype=jnp.float32), rows)
    n_half = RET_DK // 4
    freqs = ROPE_BASE ** (-jnp.arange(n_half, dtype=jnp.float32) / n_half)
    ang = jnp.concatenate([row[:, None] * freqs, col[:, None] * freqs], axis=-1)
    cos = jnp.cos(ang)[None, :, None, :]
    sin = jnp.sin(ang)[None, :, None, :]
    xf = x.astype(jnp.float32)
    x1, x2 = xf[..., : RET_DK // 2], xf[..., RET_DK // 2:]
    out = jnp.concatenate([x1 * cos - x2 * sin, x1 * sin + x2 * cos], axis=-1)
    return out.astype(x.dtype)


def _retention_dir(q, k, v, log_gamma, s0):
    q = q.astype(jnp.float32)
    k = k.astype(jnp.float32)
    v = v.astype(jnp.float32)
    B, H, L, dk = q.shape
    dv = v.shape[-1]
    nc = L // CHUNK
    qc = q.reshape(B, H, nc, CHUNK, dk)
    kc = k.reshape(B, H, nc, CHUNK, dk)
    vc = v.reshape(B, H, nc, CHUNK, dv)
    idx = jnp.arange(CHUNK, dtype=jnp.float32)
    lg = log_gamma.astype(jnp.float32)
    rel = idx[:, None] - idx[None, :]
    decay_mat = jnp.where(rel >= 0, jnp.exp(lg[:, None, None] * jnp.maximum(rel, 0.0)), 0.0)
    scores = jnp.einsum('bhnid,bhnjd->bhnij', qc, kc) * decay_mat[None, :, None]
    inner = jnp.einsum('bhnij,bhnjv->bhniv', scores, vc)
    k_decay = jnp.exp(lg[:, None] * (CHUNK - 1 - idx)[None, :])
    kv_chunk = jnp.einsum('bhnjd,hj,bhnjv->nbhdv', kc, k_decay, vc)
    chunk_decay = jnp.exp(lg * CHUNK)[None, :, None, None]

    def step(s, kv):
        return s * chunk_decay + kv, s

    s_final, s_prev = lax.scan(step, s0.astype(jnp.float32), kv_chunk)
    q_decay = jnp.exp(lg[:, None] * (idx + 1.0)[None, :])
    cross = jnp.einsum('bhnid,nbhdv,hi->bhniv', qc, s_prev, q_decay)
    o = (inner + cross).reshape(B, H, L, dv)
    return o, s_final


def _bidir_retention(q, k, v, lg_f, lg_b, s0_f, s0_b):
    o_f, s_f = _retention_dir(q, k, v, lg_f, s0_f)
    o_b, s_b = _retention_dir(jnp.flip(q, 2), jnp.flip(k, 2), jnp.flip(v, 2), lg_b, s0_b)
    return o_f + jnp.flip(o_b, 2), s_f, s_b


def _pool_mixer(u, pool_w, pool_scale):
    B, L, _ = u.shape
    uf = u.astype(jnp.float32)
    cs = jnp.concatenate([jnp.zeros((B, 1, POOL_WIDTH), jnp.float32), jnp.cumsum(uf, axis=1)], axis=1)
    t = jnp.arange(L)
    outs = []
    for gi, w in enumerate(POOL_WINDOWS):
        sl = slice(gi * POOL_GROUP, (gi + 1) * POOL_GROUP)
        csg = cs[..., sl]
        lo = jnp.clip(t - w // 2, 0, L)
        hi = jnp.clip(t + w // 2, 0, L)
        mean = (csg[:, hi] - csg[:, lo]) / (hi - lo).astype(jnp.float32)[None, :, None]
        outs.append(jnp.einsum('blc,cd->bld', mean - uf[..., sl], pool_w[gi].astype(jnp.float32)))
    out = jnp.concatenate(outs, axis=-1) * pool_scale.astype(jnp.float32)
    return out.astype(u.dtype)


def _layer(x, cond, is_latent, s0_f, s0_b, ada_w, ada_b, norm_ffn1, ffn1_w1, ffn1_w3, ffn1_w2,
           norm_mix, w_in, ret_decay_fwd, ret_decay_bwd, ret_gn, pool_w, pool_scale, w_out,
           norm_ffn2, ffn2_w1, ffn2_w3, ffn2_w2):
    B, L, _ = x.shape
    mods = (jax.nn.silu(cond.astype(jnp.float32)) @ ada_w.astype(jnp.float32) + ada_b.astype(jnp.float32)).astype(x.dtype)
    sh1, sc1, g1, sh2, sc2, g2, sh3, sc3, g3 = jnp.split(mods, N_MOD, axis=-1)

    h = x + 0.5 * g1[:, None, :] * _swiglu(_modulate(_rmsnorm(x, norm_ffn1), sh1, sc1), ffn1_w1, ffn1_w3, ffn1_w2)

    a = _modulate(_rmsnorm(h, norm_mix), sh2, sc2)
    proj = a @ w_in
    q, k, v, gate, u = jnp.split(
        proj, [RET_QK_WIDTH, 2 * RET_QK_WIDTH, 2 * RET_QK_WIDTH + RET_WIDTH, 2 * RET_QK_WIDTH + 2 * RET_WIDTH], axis=-1)
    q = q.reshape(B, L, N_RET_HEADS, RET_DK)
    k = k.reshape(B, L, N_RET_HEADS, RET_DK) * (RET_DK ** -0.5)
    if is_latent:
        q = _rotary_2d(q)
        k = _rotary_2d(k)
    q = q.transpose(0, 2, 1, 3)
    k = k.transpose(0, 2, 1, 3)
    v = v.reshape(B, L, N_RET_HEADS, RET_DV).transpose(0, 2, 1, 3)
    lg_f = -jnp.exp(ret_decay_fwd.astype(jnp.float32))
    lg_b = -jnp.exp(ret_decay_bwd.astype(jnp.float32))
    o, s_f, s_b = _bidir_retention(q, k, v, lg_f, lg_b, s0_f, s0_b)
    o = o * lax.rsqrt(jnp.mean(o * o, axis=-1, keepdims=True) + EPS)
    o = o.transpose(0, 2, 1, 3).reshape(B, L, RET_WIDTH) * ret_gn.astype(jnp.float32)
    ret_out = o.astype(x.dtype) * jax.nn.silu(gate)
    pool_out = _pool_mixer(u, pool_w, pool_scale)
    mix = jnp.concatenate([ret_out, pool_out], axis=-1) @ w_out
    h = h + g2[:, None, :] * mix

    h = h + 0.5 * g3[:, None, :] * _swiglu(_modulate(_rmsnorm(h, norm_ffn2), sh3, sc3), ffn2_w1, ffn2_w3, ffn2_w2)
    return h, s_f, s_b


def setup_inputs(seed: int = 0) -> dict:
    key = jax.random.key(seed)
    ks = jax.random.split(key, 32)
    D = D_MODEL
    f32 = jnp.float32

    def nrm(k, shape, scale):
        return jax.random.normal(k, shape, f32) * scale

    base_gamma = 1.0 - 2.0 ** (-5.0 - np.arange(N_RET_HEADS, dtype=np.float32))
    base_param = jnp.asarray(np.log(-np.log(base_gamma)), f32)
    st_shape = (DEC_BATCH, DEPTH, N_RET_HEADS, RET_DK, RET_DV)
    return {
        "x_prompt": nrm(ks[0], (BATCH, SEQ, D), 1.0),
        "x_sample": nrm(ks[1], (DEC_BATCH, DEC_SEQ, D), 1.0),
        "state_ret_fwd": nrm(ks[2], st_shape, 0.5),
        "state_ret_bwd": nrm(ks[3], st_shape, 0.5),
        "c": nrm(ks[4], (DEC_BATCH, D), 1.0),
        "c_ctx": nrm(ks[5], (D,), 1.0),
        "ada_w": nrm(ks[6], (DEPTH, D, N_MOD * D), 0.5 * D ** -0.5),
        "ada_b": nrm(ks[7], (DEPTH, N_MOD * D), 0.01),
        "norm_ffn1": 1.0 + nrm(ks[8], (DEPTH, D), 0.01),
        "ffn1_w1": nrm(ks[9], (DEPTH, D, D_FF), D ** -0.5),
        "ffn1_w3": nrm(ks[10], (DEPTH, D, D_FF), D ** -0.5),
        "ffn1_w2": nrm(ks[11], (DEPTH, D_FF, D), D_FF ** -0.5),
        "norm_mix": 1.0 + nrm(ks[12], (DEPTH, D), 0.01),
        "w_in": nrm(ks[13], (DEPTH, D, IN_WIDTH), D ** -0.5),
        "ret_decay_fwd": base_param[None, :] + nrm(ks[14], (DEPTH, N_RET_HEADS), 0.05),
        "ret_decay_bwd": base_param[None, :] + nrm(ks[15], (DEPTH, N_RET_HEADS), 0.05),
        "ret_gn": 1.0 + nrm(ks[16], (DEPTH, RET_WIDTH), 0.01),
        "pool_w": nrm(ks[17], (DEPTH, N_POOL_GROUPS, POOL_GROUP, POOL_GROUP), POOL_GROUP ** -0.5),
        "pool_scale": 1.0 + nrm(ks[18], (DEPTH, POOL_WIDTH), 0.02),
        "w_out": nrm(ks[19], (DEPTH, MIX_WIDTH, D), MIX_WIDTH ** -0.5),
        "norm_ffn2": 1.0 + nrm(ks[20], (DEPTH, D), 0.01),
        "ffn2_w1": nrm(ks[21], (DEPTH, D, D_FF), D ** -0.5),
        "ffn2_w3": nrm(ks[22], (DEPTH, D, D_FF), D ** -0.5),
        "ffn2_w2": nrm(ks[23], (DEPTH, D_FF, D), D_FF ** -0.5),
        "norm_final": 1.0 + nrm(ks[24], (D,), 0.01),
    }


def reference(x_prompt, x_sample, state_ret_fwd, state_ret_bwd, c, c_ctx, ada_w, ada_b, norm_ffn1,
              ffn1_w1, ffn1_w3, ffn1_w2, norm_mix, w_in, ret_decay_fwd, ret_decay_bwd, ret_gn, pool_w,
              pool_scale, w_out, norm_ffn2, ffn2_w1, ffn2_w3, ffn2_w2, norm_final):
    ctx = x_prompt
    lat = x_sample
    new_f = []
    new_b = []
    for l in range(DEPTH):
        p = (ada_w[l], ada_b[l], norm_ffn1[l], ffn1_w1[l], ffn1_w3[l], ffn1_w2[l], norm_mix[l], w_in[l],
             ret_decay_fwd[l], ret_decay_bwd[l], ret_gn[l], pool_w[l], pool_scale[l], w_out[l],
             norm_ffn2[l], ffn2_w1[l], ffn2_w3[l], ffn2_w2[l])
        zeros = jnp.zeros((ctx.shape[0], N_RET_HEADS, RET_DK, RET_DV), jnp.float32)
        ctx, s_f, s_b = _layer(ctx, c_ctx[None, :], False, zeros, zeros, *p)
        new_f.append(s_f.astype(x_prompt.dtype))
        new_b.append(s_b.astype(x_prompt.dtype))
        lat, _, _ = _layer(lat, c, True, state_ret_fwd[:, l], state_ret_bwd[:, l], *p)
    y_prompt = _rmsnorm(ctx, norm_final)
    y_sample = _rmsnorm(lat, norm_final)
    new_state_ret_fwd = jnp.stack(new_f, axis=1)
    new_state_ret_bwd = jnp.stack(new_b, axis=1)
    return (y_prompt, y_sample, new_state_ret_fwd, new_state_ret_bwd)
```

```python
import functools

import jax
import jax.numpy as jnp
from jax import lax
from jax.experimental import pallas as pl
from jax.experimental.pallas import tpu as pltpu

D_MODEL = 1024
GRID_W = 64
N_HEADS = 4
HEAD_DIM = 128
RET_WIDTH = N_HEADS * HEAD_DIM
POOL_WINDOWS = (2, 4, 8, 16)
POOL_HALO = 8
D_FF = 2816
CHUNK = 128
ROPE_BASE = 10000.0
N_MOD = 9
EPS = 1e-6

FF_COLS = 256
TOKEN_TILE = 512
POOL_ROWS = 512
V7X_VMEM_LIMIT = 56 * 1024 * 1024

F32 = jnp.float32
BF16 = jnp.bfloat16


def _silu(x):
    return x * (1.0 / (1.0 + jnp.exp(-x)))


def _rmsnorm(x, g):
    return x * lax.rsqrt(jnp.mean(x * x, axis=-1, keepdims=True) + EPS) * g


def _dot(a, b):
    return jnp.dot(a, b, preferred_element_type=F32)


def _mods_kernel(cond_ref, w_ref, b_ref, o_ref):
    s = _silu(cond_ref[...]).astype(BF16)
    o_ref[...] = _dot(s, w_ref[...].astype(BF16)) + b_ref[...]


def _mods(cond, ada_w, ada_b):
    n = N_MOD * D_MODEL
    out = pl.pallas_call(
        _mods_kernel,
        out_shape=jax.ShapeDtypeStruct((8, n), F32),
        grid=(N_MOD,),
        in_specs=[
            pl.BlockSpec((8, D_MODEL), lambda j: (0, 0)),
            pl.BlockSpec((D_MODEL, D_MODEL), lambda j: (0, j)),
            pl.BlockSpec((1, D_MODEL), lambda j: (0, j)),
        ],
        out_specs=pl.BlockSpec((8, D_MODEL), lambda j: (0, j)),
        name="adaln_mods",
    )(cond, ada_w, ada_b.reshape(1, n))
    return out.reshape(8, N_MOD, D_MODEL)


def _swiglu(a_bf16, w1_ref, w3_ref, w2_ref, act_ref):
    for c in range(D_FF // FF_COLS):
        cols = slice(c * FF_COLS, (c + 1) * FF_COLS)
        g = _dot(a_bf16, w1_ref[:, cols])
        u = _dot(a_bf16, w3_ref[:, cols])
        act_ref[:, cols] = (_silu(g) * u).astype(BF16)
    return _dot(act_ref[...], w2_ref[...])


def _stage_a_kernel(*refs, is_latent):
    if is_latent:
        (x_ref, mods_ref, n1_ref, w1_ref, w3_ref, w2_ref, nmix_ref, win_ref, cos_ref, sin_ref,
         h1_ref, q_ref, k_ref, v_ref, gate_ref, u_ref, act_ref) = refs
    else:
        (x_ref, mods_ref, n1_ref, w1_ref, w3_ref, w2_ref, nmix_ref, win_ref,
         h1_ref, q_ref, k_ref, v_ref, gate_ref, u_ref, act_ref) = refs
    x = x_ref[0]
    mods = mods_ref[0]
    sh1, sc1, g1, sh2, sc2 = mods[0:1], mods[1:2], mods[2:3], mods[3:4], mods[4:5]

    a1 = _rmsnorm(x, n1_ref[...]) * (1.0 + sc1) + sh1
    h1 = x + (0.5 * g1) * _swiglu(a1.astype(BF16), w1_ref, w3_ref, w2_ref, act_ref)
    h1_ref[0] = h1

    a2 = (_rmsnorm(h1, nmix_ref[...]) * (1.0 + sc2) + sh2).astype(BF16)

    def rotary(t):
        if not is_latent:
            return t
        cos2 = cos_ref[...]
        sin2 = sin_ref[...]
        outs = []
        for h in range(N_HEADS):
            th = t[:, h * HEAD_DIM:(h + 1) * HEAD_DIM]
            outs.append(th * cos2 + pltpu.roll(th, HEAD_DIM // 2, axis=1) * sin2)
        return jnp.concatenate(outs, axis=-1)

    w = RET_WIDTH
    q_ref[0] = rotary(_dot(a2, win_ref[:, 0:w])).astype(BF16)
    k_ref[0] = rotary(_dot(a2, win_ref[:, w:2 * w]) * (HEAD_DIM ** -0.5)).astype(BF16)
    v_ref[0] = _dot(a2, win_ref[:, 2 * w:3 * w]).astype(BF16)
    gate_ref[0] = _dot(a2, win_ref[:, 3 * w:4 * w])
    u_ref[0] = _dot(a2, win_ref[:, 4 * w:5 * w])


def _resident(shape):
    nd = len(shape)
    return pl.BlockSpec(shape, lambda *_: (0,) * nd, pipeline_mode=pl.Buffered(1))


def _stage_a(x, mods, mod_row0, n1, w1, w3, w2, nmix, w_in, rot):
    b, l, d = x.shape
    tm = min(TOKEN_TILE, l)
    is_latent = rot is not None
    tile = lambda width: pl.BlockSpec((1, tm, width), lambda i, j: (i, j, 0))
    in_specs = [
        tile(d),
        pl.BlockSpec((1, N_MOD, d), lambda i, j: (i + mod_row0, 0, 0)),
        _resident((1, d)), _resident(w1.shape), _resident(w3.shape), _resident(w2.shape),
        _resident((1, d)), _resident(w_in.shape),
    ]
    args = [x, mods, n1, w1, w3, w2, nmix, w_in]
    if is_latent:
        in_specs += [pl.BlockSpec((tm, HEAD_DIM), lambda i, j: (j, 0))] * 2
        args += list(rot)
    out_shape = [jax.ShapeDtypeStruct((b, l, d), F32)]
    out_shape += [jax.ShapeDtypeStruct((b, l, RET_WIDTH), BF16)] * 3
    out_shape += [jax.ShapeDtypeStruct((b, l, RET_WIDTH), F32)] * 2
    out_specs = [tile(d)] + [tile(RET_WIDTH)] * 5
    return pl.pallas_call(
        functools.partial(_stage_a_kernel, is_latent=is_latent),
        out_shape=out_shape,
        grid=(b, l // tm),
        in_specs=in_specs,
        out_specs=out_specs,
        scratch_shapes=[pltpu.VMEM((tm, D_FF), BF16)],
        compiler_params=pltpu.CompilerParams(
            dimension_semantics=("arbitrary", "arbitrary"), vmem_limit_bytes=V7X_VMEM_LIMIT),
        name="stage_a_latent" if is_latent else "stage_a_context",
    )(*args)


def _pool_block(upad_ref, t0, rows, seq_len, window):
    n_ext = rows + 2 * POOL_HALO
    ext = upad_ref[pl.ds(t0, n_ext), :]
    acc = ext
    span = 1
    while span < window:
        acc = acc + pltpu.roll(acc, span, axis=0)
        span *= 2
    lead = window // 2 - 1
    if lead:
        acc = pltpu.roll(acc, n_ext - lead, axis=0)
    win = acc[POOL_HALO:POOL_HALO + rows]
    tok = ext[POOL_HALO:POOL_HALO + rows]
    t = t0 + lax.broadcasted_iota(jnp.int32, (rows, 1), 0)
    cnt = jnp.minimum(t + window // 2, seq_len) - jnp.maximum(t - window // 2, 0)
    return win * (1.0 / cnt.astype(F32)) - tok


def _stage_b_kernel(*refs, is_latent, seq_len):
    if is_latent:
        (q_ref, k_ref, v_ref, gate_ref, u_ref, dec_f_ref, dec_b_ref, gn_ref, pw_ref, ps_ref, s0f_ref, s0b_ref,
         ret_ref, pool_ref, sb_hist_ref, upad_ref) = refs
    else:
        (q_ref, k_ref, v_ref, gate_ref, u_ref, dec_f_ref, dec_b_ref, gn_ref, pw_ref, ps_ref,
         ret_ref, pool_ref, sf_out_ref, sb_out_ref, sb_hist_ref, upad_ref) = refs
    n_chunks = seq_len // CHUNK
    c = CHUNK

    lg_f = -jnp.exp(dec_f_ref[0])
    lg_b = -jnp.exp(dec_b_ref[0])
    row = lax.broadcasted_iota(jnp.int32, (c, c), 0)
    col = lax.broadcasted_iota(jnp.int32, (c, c), 1)
    rel = (row - col).astype(F32)
    rowf = row.astype(F32)
    dmat = (jnp.where(rel >= 0, jnp.exp(lg_f * jnp.maximum(rel, 0.0)), 0.0)
            + jnp.where(rel <= 0, jnp.exp(lg_b * jnp.maximum(-rel, 0.0)), 0.0))
    q_dec_f = jnp.exp(lg_f * (rowf + 1.0))
    q_dec_b = jnp.exp(lg_b * (c - rowf))
    k_dec_f = jnp.exp(lg_f * (c - 1.0 - rowf))
    k_dec_b = jnp.exp(lg_b * rowf)
    cd_f = jnp.exp(lg_f * c)
    cd_b = jnp.exp(lg_b * c)

    def chunk(ref, n):
        return ref[0, pl.ds(pl.multiple_of(n * c, c), c), :]

    def kv_update(s, kc, vc, k_dec, cd):
        vd = (vc.astype(F32) * k_dec).astype(BF16)
        kv = lax.dot_general(kc, vd, (((0,), (0,)), ((), ())), preferred_element_type=F32)
        return s * cd + kv

    if is_latent:
        s_f0 = s0f_ref[0, 0]
        s_b0 = s0b_ref[0, 0]
    else:
        s_f0 = jnp.zeros((c, c), F32)
        s_b0 = jnp.zeros((c, c), F32)

    def bwd_body(i, s_b):
        n = n_chunks - 1 - i
        sb_hist_ref[n] = s_b.astype(BF16)
        return kv_update(s_b, chunk(k_ref, n), chunk(v_ref, n), k_dec_b, cd_b)

    s_b_final = lax.fori_loop(0, n_chunks, bwd_body, s_b0)

    gn = gn_ref[...]

    def fwd_body(n, s_f):
        qc, kc, vc = chunk(q_ref, n), chunk(k_ref, n), chunk(v_ref, n)
        qk = lax.dot_general(qc, kc, (((1,), (1,)), ((), ())), preferred_element_type=F32)
        inner = _dot((qk * dmat).astype(BF16), vc)
        cross_f = _dot(qc, s_f.astype(BF16)) * q_dec_f
        cross_b = _dot(qc, sb_hist_ref[n]) * q_dec_b
        o = inner + cross_f + cross_b
        o = o * lax.rsqrt(jnp.mean(o * o, axis=-1, keepdims=True) + EPS) * gn
        ret_ref[0, pl.ds(pl.multiple_of(n * c, c), c), :] = (o * _silu(chunk(gate_ref, n))).astype(BF16)
        return kv_update(s_f, kc, vc, k_dec_f, cd_f)

    s_f_final = lax.fori_loop(0, n_chunks, fwd_body, s_f0)

    if not is_latent:
        sf_out_ref[0, 0, 0] = s_f_final
        sb_out_ref[0, 0, 0] = s_b_final

    zeros = jnp.zeros((POOL_HALO, HEAD_DIM), F32)
    upad_ref[0:POOL_HALO, :] = zeros
    upad_ref[POOL_HALO + seq_len:, :] = zeros
    upad_ref[POOL_HALO:POOL_HALO + seq_len, :] = u_ref[0]
    pw = pw_ref[0].astype(BF16)
    ps = ps_ref[...]
    rows = min(POOL_ROWS, seq_len)
    group = pl.program_id(1)
    for gi, window in enumerate(POOL_WINDOWS):
        @pl.when(group == gi)
        def _(window=window):
            def body(r, carry):
                t0 = pl.multiple_of(r * rows, rows)
                centred = _pool_block(upad_ref, t0, rows, seq_len, window)
                pool_ref[0, pl.ds(t0, rows), :] = (_dot(centred.astype(BF16), pw) * ps).astype(BF16)
                return carry
            lax.fori_loop(0, seq_len // rows, body, 0)


def _stage_b(q, k, v, gate, u, dec_f, dec_b, gn, pool_w, pool_scale, s0):
    b, l, _ = q.shape
    is_latent = s0 is not None
    seq = lambda: pl.BlockSpec((1, l, HEAD_DIM), lambda i, h: (i, 0, h))
    lane_row = lambda: pl.BlockSpec((1, HEAD_DIM), lambda i, h: (0, h))
    state = lambda: pl.BlockSpec((1, 1, CHUNK, HEAD_DIM), lambda i, h: (i, h, 0, 0))
    in_specs = [seq(), seq(), seq(), seq(), seq(),
                pl.BlockSpec((1, 1, HEAD_DIM), lambda i, h: (h, 0, 0)),
                pl.BlockSpec((1, 1, HEAD_DIM), lambda i, h: (h, 0, 0)),
                lane_row(),
                pl.BlockSpec((1, HEAD_DIM, HEAD_DIM), lambda i, h: (h, 0, 0)),
                lane_row()]
    args = [q, k, v, gate, u, dec_f, dec_b, gn, pool_w, pool_scale]
    out_shape = [jax.ShapeDtypeStruct((b, l, RET_WIDTH), BF16)] * 2
    out_specs = [seq(), seq()]
    if is_latent:
        in_specs += [state(), state()]
        args += list(s0)
    else:
        st = jax.ShapeDtypeStruct((b, 1, N_HEADS, CHUNK, HEAD_DIM), F32)
        out_shape += [st, st]
        out_specs += [pl.BlockSpec((1, 1, 1, CHUNK, HEAD_DIM), lambda i, h: (i, 0, h, 0, 0))] * 2
    return pl.pallas_call(
        functools.partial(_stage_b_kernel, is_latent=is_latent, seq_len=l),
        out_shape=out_shape,
        grid=(b, N_HEADS),
        in_specs=in_specs,
        out_specs=out_specs,
        scratch_shapes=[pltpu.VMEM((l // CHUNK, CHUNK, HEAD_DIM), BF16),
                        pltpu.VMEM((l + 2 * POOL_HALO, HEAD_DIM), F32)],
        compiler_params=pltpu.CompilerParams(
            dimension_semantics=("arbitrary", "arbitrary"), vmem_limit_bytes=V7X_VMEM_LIMIT),
        name="stage_b_latent" if is_latent else "stage_b_context",
    )(*args)


def _stage_c_kernel(h1_ref, ret_ref, pool_ref, mods_ref, wout_ref, n2_ref, w1_ref, w3_ref, w2_ref, nf_ref,
                    y_ref, act_ref):
    mods = mods_ref[0]
    g2, sh3, sc3, g3 = mods[5:6], mods[6:7], mods[7:8], mods[8:9]
    mix = _dot(ret_ref[0], wout_ref[0:RET_WIDTH, :]) + _dot(pool_ref[0], wout_ref[RET_WIDTH:, :])
    h2 = h1_ref[0] + g2 * mix
    a3 = _rmsnorm(h2, n2_ref[...]) * (1.0 + sc3) + sh3
    h3 = h2 + (0.5 * g3) * _swiglu(a3.astype(BF16), w1_ref, w3_ref, w2_ref, act_ref)
    y_ref[0] = _rmsnorm(h3, nf_ref[...])


def _stage_c(h1, ret, pool, mods, mod_row0, w_out, n2, w1, w3, w2, nf, name):
    b, l, d = h1.shape
    tm = min(TOKEN_TILE, l)
    tile = lambda width: pl.BlockSpec((1, tm, width), lambda i, j: (i, j, 0))
    return pl.pallas_call(
        _stage_c_kernel,
        out_shape=jax.ShapeDtypeStruct((b, l, d), F32),
        grid=(b, l // tm),
        in_specs=[tile(d), tile(RET_WIDTH), tile(RET_WIDTH),
                  pl.BlockSpec((1, N_MOD, d), lambda i, j: (i + mod_row0, 0, 0)),
                  _resident(w_out.shape), _resident((1, d)),
                  _resident(w1.shape), _resident(w3.shape), _resident(w2.shape), _resident((1, d))],
        out_specs=tile(d),
        scratch_shapes=[pltpu.VMEM((tm, D_FF), BF16)],
        compiler_params=pltpu.CompilerParams(
            dimension_semantics=("arbitrary", "arbitrary"), vmem_limit_bytes=V7X_VMEM_LIMIT),
        name=name,
    )(h1, ret, pool, mods, w_out, n2, w1, w3, w2, nf)


def _rotary_tables(seq_len):
    rows = seq_len // GRID_W
    row = jnp.repeat(jnp.arange(rows, dtype=F32), GRID_W)
    col = jnp.tile(jnp.arange(GRID_W, dtype=F32), rows)
    n_half = HEAD_DIM // 4
    freqs = ROPE_BASE ** (-jnp.arange(n_half, dtype=F32) / n_half)
    ang = jnp.concatenate([row[:, None] * freqs, col[:, None] * freqs], axis=-1)
    cos, sin = jnp.cos(ang), jnp.sin(ang)
    return jnp.concatenate([cos, cos], axis=-1), jnp.concatenate([-sin, sin], axis=-1)


def kernel(x_prompt, x_sample, state_ret_fwd, state_ret_bwd, c, c_ctx, ada_w, ada_b, norm_ffn1, ffn1_w1, ffn1_w3, ffn1_w2, norm_mix, w_in, ret_decay_fwd, ret_decay_bwd, ret_gn, pool_w, pool_scale, w_out, norm_ffn2, ffn2_w1, ffn2_w3, ffn2_w2, norm_final):
    depth = ada_w.shape[0]
    assert depth == 1, "single trunk layer"
    d = D_MODEL
    n_ctx, l_ctx, _ = x_prompt.shape
    n_lat, l_lat, _ = x_sample.shape

    cond = jnp.concatenate([c_ctx[None, :], c, jnp.zeros((8 - 1 - n_lat, d), F32)], axis=0)
    mods = _mods(cond, ada_w[0], ada_b[0])

    bf = lambda w: w[0].astype(BF16)
    row = lambda g: g.reshape(1, -1)
    ffn1 = (bf(ffn1_w1), bf(ffn1_w3), bf(ffn1_w2))
    ffn2 = (bf(ffn2_w1), bf(ffn2_w3), bf(ffn2_w2))
    w_in_b, w_out_b = bf(w_in), bf(w_out)
    n1, nmix, n2, nf = row(norm_ffn1[0]), row(norm_mix[0]), row(norm_ffn2[0]), row(norm_final)
    dec_f = jnp.broadcast_to(ret_decay_fwd[0][:, None, None], (N_HEADS, 1, HEAD_DIM))
    dec_b = jnp.broadcast_to(ret_decay_bwd[0][:, None, None], (N_HEADS, 1, HEAD_DIM))
    gn, ps = row(ret_gn[0]), row(pool_scale[0])

    flat = lambda a: a.reshape(1, n_ctx * l_ctx, a.shape[-1])
    unflat = lambda a: a.reshape(n_ctx, l_ctx, a.shape[-1])
    h1c, qc, kc, vc, gc, uc = _stage_a(flat(x_prompt), mods, 0, n1, *ffn1, nmix, w_in_b, None)
    retc, poolc, new_f, new_b = _stage_b(unflat(qc), unflat(kc), unflat(vc), unflat(gc), unflat(uc),
                                         dec_f, dec_b, gn, pool_w[0], ps, None)
    y_prompt = unflat(_stage_c(h1c, flat(retc), flat(poolc), mods, 0, w_out_b, n2, *ffn2, nf, "stage_c_context"))

    h1l, ql, kl, vl, gl, ul = _stage_a(x_sample, mods, 1, n1, *ffn1, nmix, w_in_b, _rotary_tables(l_lat))
    retl, pooll = _stage_b(ql, kl, vl, gl, ul, dec_f, dec_b, gn, pool_w[0], ps,
                           (state_ret_fwd[:, 0], state_ret_bwd[:, 0]))
    y_sample = _stage_c(h1l, retl, pooll, mods, 1, w_out_b, n2, *ffn2, nf, "stage_c_latent")

    return (y_prompt, y_sample, new_f, new_b)
```

```python
import functools

import jax
import jax.numpy as jnp
from jax import lax
from jax.experimental import pallas as pl
from jax.experimental.pallas import tpu as pltpu

D_MODEL = 1024
GRID_W = 64
N_HEADS = 4
HEAD_DIM = 128
RET_WIDTH = N_HEADS * HEAD_DIM
POOL_WINDOWS = (2, 4, 8, 16)
POOL_HALO = 8
D_FF = 2816
ROPE_BASE = 10000.0
N_MOD = 9
EPS = 1e-6

FF_COLS = 256
TOKEN_TILE = 512
RET_CHUNK = 256
STAGE_DEPTH = 2
V7X_VMEM_LIMIT = 56 * 1024 * 1024

F32 = jnp.float32
BF16 = jnp.bfloat16


def _silu(x):
    return x * (1.0 / (1.0 + jnp.exp(-x)))


def _rmsnorm(x, g):
    return x * lax.rsqrt(jnp.mean(x * x, axis=-1, keepdims=True) + EPS) * g


def _dot(a, b):
    return jnp.dot(a, b, preferred_element_type=F32)


def _dot_tn(a, b):
    return lax.dot_general(a, b, (((0,), (0,)), ((), ())), preferred_element_type=F32)


def _dot_nt(a, b):
    return lax.dot_general(a, b, (((1,), (1,)), ((), ())), preferred_element_type=F32)


def _head(h):
    return slice(h * HEAD_DIM, (h + 1) * HEAD_DIM)


def _mods_kernel(cond_ref, w_ref, b_ref, o_ref):
    s = _silu(cond_ref[...]).astype(BF16)
    o_ref[...] = _dot(s, w_ref[...].astype(BF16)) + b_ref[...]


def _mods(cond, ada_w, ada_b):
    n = N_MOD * D_MODEL
    out = pl.pallas_call(
        _mods_kernel,
        out_shape=jax.ShapeDtypeStruct((8, n), F32),
        grid=(N_MOD,),
        in_specs=[
            pl.BlockSpec((8, D_MODEL), lambda j: (0, 0)),
            pl.BlockSpec((D_MODEL, D_MODEL), lambda j: (0, j)),
            pl.BlockSpec((1, D_MODEL), lambda j: (0, j)),
        ],
        out_specs=pl.BlockSpec((8, D_MODEL), lambda j: (0, j)),
        name="adaln_mods",
    )(cond, ada_w, ada_b.reshape(1, n))
    return out.reshape(8, N_MOD, D_MODEL)


def _swiglu(a_bf16, w1_ref, w3_ref, w2_ref, act_ref, take_panels=None):
    for c in range(D_FF // FF_COLS):
        if take_panels is not None:
            take_panels(c)
        cols = slice(c * FF_COLS, (c + 1) * FF_COLS)
        g = _dot(a_bf16, w1_ref[:, cols])
        u = _dot(a_bf16, w3_ref[:, cols])
        act_ref[:, cols] = (_silu(g) * u).astype(BF16)
    return _dot(act_ref[...], w2_ref[...])


def _panel_stream(srcs, dsts, stage_ref, sem_ref):
    depth = stage_ref.shape[0]

    def copy(k):
        return pltpu.make_async_copy(srcs[k], stage_ref.at[k % depth], sem_ref.at[k % depth])

    def prime():
        for k in range(min(depth, len(srcs))):
            copy(k).start()

    def take(k):
        copy(k).wait()
        dsts[k][...] = stage_ref[k % depth].astype(BF16)
        if k + depth < len(srcs):
            copy(k + depth).start()

    return prime, take


def _col_panels(ref):
    return [ref.at[:, pl.ds(p * FF_COLS, FF_COLS)] for p in range(ref.shape[1] // FF_COLS)]


def _row_panels(ref):
    return [ref.at[pl.ds(p * FF_COLS, FF_COLS), :] for p in range(ref.shape[0] // FF_COLS)]


def _ffn_streams(w1_hbm, w3_hbm, w2_hbm, w1_ref, w3_ref, w2_ref, extra_src, extra_dst, extra_first,
                 col_stage, col_sem, row_stage, row_sem):
    interleave = lambda a, b: [x for pair in zip(a, b) for x in pair]
    ffn_src = interleave(_col_panels(w1_hbm), _col_panels(w3_hbm))
    ffn_dst = interleave(_col_panels(w1_ref), _col_panels(w3_ref))
    n_extra = len(extra_src)
    if extra_first:
        srcs, dsts, ffn0, extra0 = extra_src + ffn_src, extra_dst + ffn_dst, n_extra, 0
    else:
        srcs, dsts, ffn0, extra0 = ffn_src + extra_src, ffn_dst + extra_dst, 0, len(ffn_src)
    col_prime, col_take = _panel_stream(srcs, dsts, col_stage, col_sem)
    row_prime, row_take = _panel_stream(_row_panels(w2_hbm), _row_panels(w2_ref), row_stage, row_sem)

    def prime():
        col_prime()
        row_prime()

    def take_extra():
        for p in range(n_extra):
            col_take(extra0 + p)

    def take_ffn_panel(c):
        col_take(ffn0 + 2 * c)
        col_take(ffn0 + 2 * c + 1)
        row_take(c)

    return prime, take_extra, take_ffn_panel


def _weight_scratch(extra_cols):
    return [pltpu.VMEM((D_MODEL, D_FF), BF16), pltpu.VMEM((D_MODEL, D_FF), BF16), pltpu.VMEM((D_FF, D_MODEL), BF16),
            pltpu.VMEM((D_MODEL, extra_cols), BF16),
            pltpu.VMEM((STAGE_DEPTH, D_MODEL, FF_COLS), F32), pltpu.SemaphoreType.DMA((STAGE_DEPTH,)),
            pltpu.VMEM((STAGE_DEPTH, FF_COLS, D_MODEL), F32), pltpu.SemaphoreType.DMA((STAGE_DEPTH,))]


def _run_first_and_rest(body):
    first = (pl.program_id(0) == 0) & (pl.program_id(1) == 0)
    pl.when(first)(lambda: body(True))
    pl.when(jnp.logical_not(first))(lambda: body(False))


def _stage_a_kernel(*refs, is_latent):
    if is_latent:
        (x_ref, mods_ref, n1_ref, w1_hbm, w3_hbm, w2_hbm, nmix_ref, win_hbm, cos_ref, sin_ref,
         h1_ref, q_ref, k_ref, v_ref, gate_ref, u_ref,
         act_ref, w1_ref, w3_ref, w2_ref, win_ref, col_stage, col_sem, row_stage, row_sem) = refs
    else:
        (x_ref, mods_ref, n1_ref, w1_hbm, w3_hbm, w2_hbm, nmix_ref, win_hbm,
         h1_ref, q_ref, k_ref, v_ref, gate_ref, u_ref,
         act_ref, w1_ref, w3_ref, w2_ref, win_ref, col_stage, col_sem, row_stage, row_sem) = refs

    def rotary(t):
        if not is_latent:
            return t
        cos2 = cos_ref[...]
        sin2 = sin_ref[...]
        outs = []
        for h in range(N_HEADS):
            th = t[:, _head(h)]
            outs.append(th * cos2 + pltpu.roll(th, HEAD_DIM // 2, axis=1) * sin2)
        return jnp.concatenate(outs, axis=-1)

    def body(streaming):
        take_win = take_ffn = None
        if streaming:
            prime, take_win, take_ffn = _ffn_streams(
                w1_hbm, w3_hbm, w2_hbm, w1_ref, w3_ref, w2_ref, _col_panels(win_hbm), _col_panels(win_ref), False,
                col_stage, col_sem, row_stage, row_sem)
            prime()
        x = x_ref[0]
        mods = mods_ref[0]
        sh1, sc1, g1, sh2, sc2 = mods[0:1], mods[1:2], mods[2:3], mods[3:4], mods[4:5]

        a1 = _rmsnorm(x, n1_ref[...]) * (1.0 + sc1) + sh1
        h1 = x + (0.5 * g1) * _swiglu(a1.astype(BF16), w1_ref, w3_ref, w2_ref, act_ref, take_ffn)
        h1_ref[0] = h1
        if streaming:
            take_win()

        a2 = (_rmsnorm(h1, nmix_ref[...]) * (1.0 + sc2) + sh2).astype(BF16)
        w = RET_WIDTH
        q_ref[0] = rotary(_dot(a2, win_ref[:, 0:w])).astype(BF16)
        k_ref[0] = rotary(_dot(a2, win_ref[:, w:2 * w]) * (HEAD_DIM ** -0.5)).astype(BF16)
        v_ref[0] = _dot(a2, win_ref[:, 2 * w:3 * w]).astype(BF16)
        gate_ref[0] = _dot(a2, win_ref[:, 3 * w:4 * w])
        u_ref[0] = _dot(a2, win_ref[:, 4 * w:5 * w])

    _run_first_and_rest(body)


def _resident(shape):
    nd = len(shape)
    return pl.BlockSpec(shape, lambda *_: (0,) * nd, pipeline_mode=pl.Buffered(1))


def _hbm():
    return pl.BlockSpec(memory_space=pl.ANY)


def _stage_a(x, mods, mod_row0, n1, w1, w3, w2, nmix, w_in, rot):
    b, l, d = x.shape
    tm = min(TOKEN_TILE, l)
    is_latent = rot is not None
    tile = lambda width: pl.BlockSpec((1, tm, width), lambda i, j: (i, j, 0))
    in_specs = [
        tile(d),
        pl.BlockSpec((1, N_MOD, d), lambda i, j: (i + mod_row0, 0, 0)),
        _resident((1, d)), _hbm(), _hbm(), _hbm(),
        _resident((1, d)), _hbm(),
    ]
    args = [x, mods, n1, w1, w3, w2, nmix, w_in]
    if is_latent:
        in_specs += [pl.BlockSpec((tm, HEAD_DIM), lambda i, j: (j, 0))] * 2
        args += list(rot)
    out_shape = [jax.ShapeDtypeStruct((b, l, d), F32)]
    out_shape += [jax.ShapeDtypeStruct((b, l, RET_WIDTH), BF16)] * 3
    out_shape += [jax.ShapeDtypeStruct((b, l, RET_WIDTH), F32)] * 2
    out_specs = [tile(d)] + [tile(RET_WIDTH)] * 5
    return pl.pallas_call(
        functools.partial(_stage_a_kernel, is_latent=is_latent),
        out_shape=out_shape,
        grid=(b, l // tm),
        in_specs=in_specs,
        out_specs=out_specs,
        scratch_shapes=[pltpu.VMEM((tm, D_FF), BF16)] + _weight_scratch(w_in.shape[1]),
        compiler_params=pltpu.CompilerParams(
            dimension_semantics=("arbitrary", "arbitrary"), vmem_limit_bytes=V7X_VMEM_LIMIT),
        name="stage_a_latent" if is_latent else "stage_a_context",
    )(*args)


def _ret_tables(dec_f_ref, dec_b_ref, dmat_ref, qd_ref, kd_ref, cd_ref):
    c = RET_CHUNK
    row = lax.broadcasted_iota(jnp.int32, (c, c), 0)
    col = lax.broadcasted_iota(jnp.int32, (c, c), 1)
    rel = (row - col).astype(F32)
    r = row[:, :HEAD_DIM].astype(F32)
    for h in range(N_HEADS):
        lg_f = -jnp.exp(dec_f_ref[h])
        lg_b = -jnp.exp(dec_b_ref[h])
        dmat_ref[h] = (jnp.where(rel >= 0, jnp.exp(lg_f * jnp.maximum(rel, 0.0)), 0.0)
                       + jnp.where(rel <= 0, jnp.exp(lg_b * jnp.maximum(-rel, 0.0)), 0.0))
        lf, lb = lg_f[:, :HEAD_DIM], lg_b[:, :HEAD_DIM]
        qd_ref[h] = jnp.concatenate([jnp.exp(lf * (r + 1.0)), jnp.exp(lb * (c - r))], axis=1)
        kd_ref[h] = jnp.concatenate([jnp.exp(lf * (c - 1.0 - r)), jnp.exp(lb * r)], axis=1)
        cd_ref[h] = jnp.concatenate([jnp.exp(lf * c), jnp.exp(lb * c)], axis=1)


def _chunk_kv(k_c, v_c, kd):
    v32 = v_c.astype(F32)
    vd = (jnp.concatenate([v32, v32], axis=1) * kd).astype(BF16)
    return _dot_tn(k_c, vd)


def _ret_out(q_c, k_c, v_c, dmat, cross, gate_c, gn):
    p = (_dot_nt(q_c, k_c) * dmat).astype(BF16)
    o = _dot(p, v_c)
    if cross is not None:
        o = o + cross[:, :HEAD_DIM] + cross[:, HEAD_DIM:]
    o = o * lax.rsqrt(jnp.mean(o * o, axis=-1, keepdims=True) + EPS) * gn
    return (o * _silu(gate_c)).astype(BF16)


def _pool_centred(ext, t0, rows, seq_len, window):
    n_ext = rows + 2 * POOL_HALO
    acc = ext
    span = 1
    while span < window:
        acc = acc + pltpu.roll(acc, span, axis=0)
        span *= 2
    lead = window // 2 - 1
    if lead:
        acc = pltpu.roll(acc, n_ext - lead, axis=0)
    win = acc[POOL_HALO:POOL_HALO + rows]
    tok = ext[POOL_HALO:POOL_HALO + rows]
    t = t0 + lax.broadcasted_iota(jnp.int32, (rows, 1), 0)
    cnt = jnp.minimum(t + window // 2, seq_len) - jnp.maximum(t - window // 2, 0)
    return win * (1.0 / cnt.astype(F32)) - tok


def _pool_groups(ext_ref, pw_ref, ps_ref, t0, rows, seq_len, store):
    for g, window in enumerate(POOL_WINDOWS):
        centred = _pool_centred(ext_ref[:, _head(g)], t0, rows, seq_len, window)
        store(g, (_dot(centred.astype(BF16), pw_ref[g].astype(BF16)) * ps_ref[:, _head(g)]).astype(BF16))


def _stage_b_context_kernel(q_ref, k_ref, v_ref, gate_ref, u_ref, dec_f_ref, dec_b_ref, gn_ref, pw_ref, ps_ref,
                            ret_ref, pool_ref, sf_ref, sb_ref, dmat_ref, qd_ref, kd_ref, cd_ref, ext_ref,
                            *, seq_len):
    @pl.when(pl.program_id(0) == 0)
    def _():
        _ret_tables(dec_f_ref, dec_b_ref, dmat_ref, qd_ref, kd_ref, cd_ref)
        ext_ref[0:POOL_HALO, :] = jnp.zeros((POOL_HALO, RET_WIDTH), F32)
        ext_ref[POOL_HALO + seq_len:, :] = jnp.zeros((POOL_HALO, RET_WIDTH), F32)

    for s in range(q_ref.shape[0] // seq_len):
        rows = slice(s * seq_len, (s + 1) * seq_len)
        for h in range(N_HEADS):
            q_c, k_c, v_c = q_ref[rows, _head(h)], k_ref[rows, _head(h)], v_ref[rows, _head(h)]
            ret_ref[rows, _head(h)] = _ret_out(q_c, k_c, v_c, dmat_ref[h], None, gate_ref[rows, _head(h)],
                                               gn_ref[:, _head(h)])
            kv = _chunk_kv(k_c, v_c, kd_ref[h])
            sf_ref[s, 0, h] = kv[:, :HEAD_DIM]
            sb_ref[s, 0, h] = kv[:, HEAD_DIM:]
        ext_ref[POOL_HALO:POOL_HALO + seq_len, :] = u_ref[rows, :]

        def store(g, val, rows=rows):
            pool_ref[rows, _head(g)] = val
        _pool_groups(ext_ref, pw_ref, ps_ref, 0, seq_len, seq_len, store)


def _stage_b_latent_kernel(q_ref, k_ref, v_ref, gate_ref, u_ref, uprev_ref, unext_ref, dec_f_ref, dec_b_ref,
                           gn_ref, pw_ref, ps_ref, s0f_ref, s0b_ref,
                           ret_ref, pool_ref,
                           dmat_ref, qd_ref, kd_ref, cd_ref, ext_ref, sf_ref, sb_ref, sb_hist_ref, kvf_hist_ref,
                           *, seq_len, n_tiles):
    c = RET_CHUNK
    tl = q_ref.shape[1]
    chunks = tl // c
    j = pl.program_id(1)

    @pl.when((pl.program_id(0) == 0) & (j == 0))
    def _():
        _ret_tables(dec_f_ref, dec_b_ref, dmat_ref, qd_ref, kd_ref, cd_ref)

    @pl.when(j == 0)
    def _():
        for h in range(N_HEADS):
            sb_ref[h] = s0b_ref[0, h]
            sf_ref[h] = s0f_ref[0, h]

    @pl.when(j < n_tiles)
    def _():
        tile = n_tiles - 1 - j
        for ci in reversed(range(chunks)):
            n = tile * chunks + ci
            rows = slice(ci * c, (ci + 1) * c)
            for h in range(N_HEADS):
                kv = _chunk_kv(k_ref[0, rows, _head(h)], v_ref[0, rows, _head(h)], kd_ref[h])
                s_b = sb_ref[h]
                sb_hist_ref[n, h] = s_b.astype(BF16)
                kvf_hist_ref[n, h] = kv[:, :HEAD_DIM]
                sb_ref[h] = s_b * cd_ref[h][:, HEAD_DIM:] + kv[:, HEAD_DIM:]

    @pl.when(j >= n_tiles)
    def _():
        tile = j - n_tiles
        for ci in range(chunks):
            n = tile * chunks + ci
            rows = slice(ci * c, (ci + 1) * c)
            for h in range(N_HEADS):
                q_c = q_ref[0, rows, _head(h)]
                s_f = sf_ref[h]
                s2 = jnp.concatenate([s_f.astype(BF16), sb_hist_ref[n, h]], axis=1)
                cross = _dot(q_c, s2) * qd_ref[h]
                ret_ref[0, rows, _head(h)] = _ret_out(q_c, k_ref[0, rows, _head(h)], v_ref[0, rows, _head(h)],
                                                      dmat_ref[h], cross, gate_ref[0, rows, _head(h)],
                                                      gn_ref[:, _head(h)])
                sf_ref[h] = s_f * cd_ref[h][:, :HEAD_DIM] + kvf_hist_ref[n, h]

        zeros = jnp.zeros((POOL_HALO, RET_WIDTH), F32)
        ext_ref[0:POOL_HALO, :] = jnp.where(tile > 0, uprev_ref[0], zeros)
        ext_ref[POOL_HALO:POOL_HALO + tl, :] = u_ref[0]
        ext_ref[POOL_HALO + tl:, :] = jnp.where(tile < n_tiles - 1, unext_ref[0], zeros)

        def store(g, val):
            pool_ref[0, :, _head(g)] = val
        _pool_groups(ext_ref, pw_ref, ps_ref, tile * tl, tl, seq_len, store)


def _ret_table_scratch():
    c = RET_CHUNK
    return [pltpu.VMEM((N_HEADS, c, c), F32), pltpu.VMEM((N_HEADS, c, 2 * HEAD_DIM), F32),
            pltpu.VMEM((N_HEADS, c, 2 * HEAD_DIM), F32), pltpu.VMEM((N_HEADS, 1, 2 * HEAD_DIM), F32)]


def _stage_b_context(q, k, v, gate, u, dec_f, dec_b, gn, pool_w, pool_scale, n_seq, seq_len):
    assert seq_len == RET_CHUNK
    tokens = n_seq * seq_len
    tl = TOKEN_TILE
    seqs = tl // seq_len
    tile = lambda: pl.BlockSpec((tl, RET_WIDTH), lambda i: (i, 0))
    state_shape = jax.ShapeDtypeStruct((n_seq, 1, N_HEADS, HEAD_DIM, HEAD_DIM), F32)
    state_spec = lambda: pl.BlockSpec((seqs, 1, N_HEADS, HEAD_DIM, HEAD_DIM), lambda i: (i, 0, 0, 0, 0))
    return pl.pallas_call(
        functools.partial(_stage_b_context_kernel, seq_len=seq_len),
        out_shape=[jax.ShapeDtypeStruct((tokens, RET_WIDTH), BF16)] * 2 + [state_shape] * 2,
        grid=(tokens // tl,),
        in_specs=[tile(), tile(), tile(), tile(), tile(),
                  _resident(dec_f.shape), _resident(dec_b.shape), _resident(gn.shape),
                  _resident(pool_w.shape), _resident(pool_scale.shape)],
        out_specs=[tile(), tile(), state_spec(), state_spec()],
        scratch_shapes=_ret_table_scratch() + [pltpu.VMEM((seq_len + 2 * POOL_HALO, RET_WIDTH), F32)],
        compiler_params=pltpu.CompilerParams(
            dimension_semantics=("arbitrary",), vmem_limit_bytes=V7X_VMEM_LIMIT),
        name="stage_b_context",
    )(q, k, v, gate, u, dec_f, dec_b, gn, pool_w, pool_scale)


def _stage_b_latent(q, k, v, gate, u, dec_f, dec_b, gn, pool_w, pool_scale, s0f, s0b):
    b, l, _ = q.shape
    tl = TOKEN_TILE
    nt = l // tl
    n_chunks = l // RET_CHUNK
    hb = tl // POOL_HALO
    fwd_tile = lambda j: jnp.maximum(j - nt, 0)
    both_tile = lambda j: jnp.where(j < nt, nt - 1 - j, j - nt)
    fwd = lambda: pl.BlockSpec((1, tl, RET_WIDTH), lambda i, j: (i, fwd_tile(j), 0))
    both = lambda: pl.BlockSpec((1, tl, RET_WIDTH), lambda i, j: (i, both_tile(j), 0))
    halo_prev = pl.BlockSpec((1, POOL_HALO, RET_WIDTH), lambda i, j: (i, jnp.maximum(fwd_tile(j) * hb - 1, 0), 0))
    halo_next = pl.BlockSpec((1, POOL_HALO, RET_WIDTH),
                             lambda i, j: (i, jnp.minimum((fwd_tile(j) + 1) * hb, l // POOL_HALO - 1), 0))
    state = lambda: pl.BlockSpec((1, N_HEADS, HEAD_DIM, HEAD_DIM), lambda i, j: (i, 0, 0, 0))
    return pl.pallas_call(
        functools.partial(_stage_b_latent_kernel, seq_len=l, n_tiles=nt),
        out_shape=[jax.ShapeDtypeStruct((b, l, RET_WIDTH), BF16)] * 2,
        grid=(b, 2 * nt),
        in_specs=[fwd(), both(), both(), fwd(), fwd(), halo_prev, halo_next,
                  _resident(dec_f.shape), _resident(dec_b.shape), _resident(gn.shape),
                  _resident(pool_w.shape), _resident(pool_scale.shape), state(), state()],
        out_specs=[fwd(), fwd()],
        scratch_shapes=_ret_table_scratch() + [
            pltpu.VMEM((tl + 2 * POOL_HALO, RET_WIDTH), F32),
            pltpu.VMEM((N_HEADS, HEAD_DIM, HEAD_DIM), F32), pltpu.VMEM((N_HEADS, HEAD_DIM, HEAD_DIM), F32),
            pltpu.VMEM((n_chunks, N_HEADS, HEAD_DIM, HEAD_DIM), BF16),
            pltpu.VMEM((n_chunks, N_HEADS, HEAD_DIM, HEAD_DIM), F32)],
        compiler_params=pltpu.CompilerParams(
            dimension_semantics=("arbitrary", "arbitrary"), vmem_limit_bytes=V7X_VMEM_LIMIT),
        name="stage_b_latent",
    )(q, k, v, gate, u, u, u, dec_f, dec_b, gn, pool_w, pool_scale, s0f, s0b)


def _stage_c_kernel(h1_ref, ret_ref, pool_ref, mods_ref, wout_hbm, n2_ref, w1_hbm, w3_hbm, w2_hbm, nf_ref,
                    y_ref,
                    act_ref, w1_ref, w3_ref, w2_ref, wout_ref, col_stage, col_sem, row_stage, row_sem):
    def body(streaming):
        take_ffn = None
        if streaming:
            prime, take_wout, take_ffn = _ffn_streams(
                w1_hbm, w3_hbm, w2_hbm, w1_ref, w3_ref, w2_ref, _col_panels(wout_hbm), _col_panels(wout_ref), True,
                col_stage, col_sem, row_stage, row_sem)
            prime()
            take_wout()
        mods = mods_ref[0]
        g2, sh3, sc3, g3 = mods[5:6], mods[6:7], mods[7:8], mods[8:9]
        mix = _dot(ret_ref[0], wout_ref[0:RET_WIDTH, :]) + _dot(pool_ref[0], wout_ref[RET_WIDTH:, :])
        h2 = h1_ref[0] + g2 * mix
        a3 = _rmsnorm(h2, n2_ref[...]) * (1.0 + sc3) + sh3
        h3 = h2 + (0.5 * g3) * _swiglu(a3.astype(BF16), w1_ref, w3_ref, w2_ref, act_ref, take_ffn)
        y_ref[0] = _rmsnorm(h3, nf_ref[...])

    _run_first_and_rest(body)


def _stage_c(h1, ret, pool, mods, mod_row0, w_out, n2, w1, w3, w2, nf, name):
    b, l, d = h1.shape
    tm = min(TOKEN_TILE, l)
    tile = lambda width: pl.BlockSpec((1, tm, width), lambda i, j: (i, j, 0))
    return pl.pallas_call(
        _stage_c_kernel,
        out_shape=jax.ShapeDtypeStruct((b, l, d), F32),
        grid=(b, l // tm),
        in_specs=[tile(d), tile(RET_WIDTH), tile(RET_WIDTH),
                  pl.BlockSpec((1, N_MOD, d), lambda i, j: (i + mod_row0, 0, 0)),
                  _hbm(), _resident((1, d)), _hbm(), _hbm(), _hbm(), _resident((1, d))],
        out_specs=tile(d),
        scratch_shapes=[pltpu.VMEM((tm, D_FF), BF16)] + _weight_scratch(w_out.shape[1]),
        compiler_params=pltpu.CompilerParams(
            dimension_semantics=("arbitrary", "arbitrary"), vmem_limit_bytes=V7X_VMEM_LIMIT),
        name=name,
    )(h1, ret, pool, mods, w_out, n2, w1, w3, w2, nf)


def _rotary_tables(seq_len):
    rows = seq_len // GRID_W
    row = jnp.repeat(jnp.arange(rows, dtype=F32), GRID_W)
    col = jnp.tile(jnp.arange(GRID_W, dtype=F32), rows)
    n_half = HEAD_DIM // 4
    freqs = ROPE_BASE ** (-jnp.arange(n_half, dtype=F32) / n_half)
    ang = jnp.concatenate([row[:, None] * freqs, col[:, None] * freqs], axis=-1)
    cos, sin = jnp.cos(ang), jnp.sin(ang)
    return jnp.concatenate([cos, cos], axis=-1), jnp.concatenate([-sin, sin], axis=-1)


def kernel(x_prompt, x_sample, state_ret_fwd, state_ret_bwd, c, c_ctx, ada_w, ada_b, norm_ffn1, ffn1_w1, ffn1_w3, ffn1_w2, norm_mix, w_in, ret_decay_fwd, ret_decay_bwd, ret_gn, pool_w, pool_scale, w_out, norm_ffn2, ffn2_w1, ffn2_w3, ffn2_w2, norm_final):
    depth = ada_w.shape[0]
    assert depth == 1, "single trunk layer"
    d = D_MODEL
    n_ctx, l_ctx, _ = x_prompt.shape
    n_lat, l_lat, _ = x_sample.shape

    cond = jnp.concatenate([c_ctx[None, :], c, jnp.zeros((8 - 1 - n_lat, d), F32)], axis=0)
    mods = _mods(cond, ada_w[0], ada_b[0])

    row = lambda g: g.reshape(1, -1)
    ffn1 = (ffn1_w1[0], ffn1_w3[0], ffn1_w2[0])
    ffn2 = (ffn2_w1[0], ffn2_w3[0], ffn2_w2[0])
    w_in_b, w_out_b = w_in[0], w_out[0]
    n1, nmix, n2, nf = row(norm_ffn1[0]), row(norm_mix[0]), row(norm_ffn2[0]), row(norm_final)
    dec_f = jnp.broadcast_to(ret_decay_fwd[0][:, None, None], (N_HEADS, 1, RET_CHUNK))
    dec_b = jnp.broadcast_to(ret_decay_bwd[0][:, None, None], (N_HEADS, 1, RET_CHUNK))
    gn, ps = row(ret_gn[0]), row(pool_scale[0])

    flat = lambda a: a.reshape(1, n_ctx * l_ctx, a.shape[-1])
    flat2 = lambda a: a.reshape(n_ctx * l_ctx, a.shape[-1])
    h1c, qc, kc, vc, gc, uc = _stage_a(flat(x_prompt), mods, 0, n1, *ffn1, nmix, w_in_b, None)
    retc, poolc, new_f, new_b = _stage_b_context(flat2(qc), flat2(kc), flat2(vc), flat2(gc), flat2(uc),
                                                 dec_f, dec_b, gn, pool_w[0], ps, n_ctx, l_ctx)
    y_prompt = _stage_c(h1c, flat(retc), flat(poolc), mods, 0, w_out_b, n2, *ffn2, nf, "stage_c_context")
    y_prompt = y_prompt.reshape(n_ctx, l_ctx, d)

    h1l, ql, kl, vl, gl, ul = _stage_a(x_sample, mods, 1, n1, *ffn1, nmix, w_in_b, _rotary_tables(l_lat))
    retl, pooll = _stage_b_latent(ql, kl, vl, gl, ul, dec_f, dec_b, gn, pool_w[0], ps,
                                  state_ret_fwd[:, 0], state_ret_bwd[:, 0])
    y_sample = _stage_c(h1l, retl, pooll, mods, 1, w_out_b, n2, *ffn2, nf, "stage_c_latent")

    return (y_prompt, y_sample, new_f, new_b)
```

```python
import functools

import jax
import jax.numpy as jnp
from jax import lax
from jax.experimental import pallas as pl
from jax.experimental.pallas import tpu as pltpu

D_MODEL = 1024
GRID_W = 64
N_HEADS = 4
HEAD_DIM = 128
RET_WIDTH = N_HEADS * HEAD_DIM
POOL_WINDOWS = (2, 4, 8, 16)
POOL_HALO = 8
D_FF = 2816
ROPE_BASE = 10000.0
N_MOD = 9
EPS = 1e-6

FF_COLS = 256
SUB_TILE = 512
FFN_SUB_TILES = 2
RET_TILE = 512
RET_CHUNK = 256
COL_STAGE_DEPTH = 4
ROW_STAGE_DEPTH = 2
PANEL_SPLIT = 4
V7X_VMEM_LIMIT = 56 * 1024 * 1024

F32 = jnp.float32
BF16 = jnp.bfloat16


def _silu(x):
    return x * (1.0 / (1.0 + jnp.exp(-x)))


def _rmsnorm(x, g):
    return x * lax.rsqrt(jnp.mean(x * x, axis=-1, keepdims=True) + EPS) * g


def _dot(a, b):
    return jnp.dot(a, b, preferred_element_type=F32)


def _dot_tn(a, b):
    return lax.dot_general(a, b, (((0,), (0,)), ((), ())), preferred_element_type=F32)


def _dot_nt(a, b):
    return lax.dot_general(a, b, (((1,), (1,)), ((), ())), preferred_element_type=F32)


def _head(h):
    return slice(h * HEAD_DIM, (h + 1) * HEAD_DIM)


def _resident(shape):
    nd = len(shape)
    return pl.BlockSpec(shape, lambda *_: (0,) * nd, pipeline_mode=pl.Buffered(1))


def _hbm():
    return pl.BlockSpec(memory_space=pl.ANY)


def _mods_kernel(cond_ref, w_ref, b_ref, o_ref):
    s = _silu(cond_ref[...]).astype(BF16)
    o_ref[...] = _dot(s, w_ref[...].astype(BF16)) + b_ref[...]


def _mods(cond, ada_w, ada_b):
    n = N_MOD * D_MODEL
    out = pl.pallas_call(
        _mods_kernel,
        out_shape=jax.ShapeDtypeStruct((8, n), F32),
        grid=(N_MOD,),
        in_specs=[
            pl.BlockSpec((8, D_MODEL), lambda j: (0, 0)),
            pl.BlockSpec((D_MODEL, D_MODEL), lambda j: (0, j)),
            pl.BlockSpec((1, D_MODEL), lambda j: (0, j)),
        ],
        out_specs=pl.BlockSpec((8, D_MODEL), lambda j: (0, j)),
        name="adaln_mods",
    )(cond, ada_w, ada_b.reshape(1, n))
    return out.reshape(8, N_MOD, D_MODEL)


def _swiglu(a_bf16, w1_ref, w3_ref, w2_ref, act_ref):
    for c in range(D_FF // FF_COLS):
        cols = slice(c * FF_COLS, (c + 1) * FF_COLS)
        g = _dot(a_bf16, w1_ref[:, cols])
        u = _dot(a_bf16, w3_ref[:, cols])
        act_ref[:, cols] = (_silu(g) * u).astype(BF16)
    return _dot(act_ref[...], w2_ref[...])


def _panel_stream(srcs, dsts, stage_ref, sem_ref):
    depth, rows = stage_ref.shape[0], stage_ref.shape[1]
    part = rows // PANEL_SPLIT

    def copies(k):
        slot = k % depth
        return [pltpu.make_async_copy(srcs[k].at[pl.ds(s * part, part), :],
                                      stage_ref.at[slot, pl.ds(s * part, part), :], sem_ref.at[slot])
                for s in range(PANEL_SPLIT)]

    def start(k):
        for cp in copies(k):
            cp.start()

    def prime():
        for k in range(min(depth, len(srcs))):
            start(k)

    def take(k):
        for cp in copies(k):
            cp.wait()
        dsts[k][...] = stage_ref[k % depth].astype(BF16)
        if k + depth < len(srcs):
            start(k + depth)

    return prime, take


def _col_panels(ref):
    return [ref.at[:, pl.ds(p * FF_COLS, FF_COLS)] for p in range(ref.shape[1] // FF_COLS)]


def _row_panels(ref):
    return [ref.at[pl.ds(p * FF_COLS, FF_COLS), :] for p in range(ref.shape[0] // FF_COLS)]


def _load_weights(col_pairs, row_pairs, col_stage, col_sem, row_stage, row_sem):
    col_src = [p for src, _ in col_pairs for p in _col_panels(src)]
    col_dst = [p for _, dst in col_pairs for p in _col_panels(dst)]
    row_src = [p for src, _ in row_pairs for p in _row_panels(src)]
    row_dst = [p for _, dst in row_pairs for p in _row_panels(dst)]
    col_prime, col_take = _panel_stream(col_src, col_dst, col_stage, col_sem)
    row_prime, row_take = _panel_stream(row_src, row_dst, row_stage, row_sem)
    col_prime()
    row_prime()
    n_col, n_row = len(col_src), len(row_src)
    for k in range(max(n_col, n_row)):
        for kc in range(k * n_col // max(n_col, n_row), (k + 1) * n_col // max(n_col, n_row)):
            col_take(kc)
        for kr in range(k * n_row // max(n_col, n_row), (k + 1) * n_row // max(n_col, n_row)):
            row_take(kr)


def _ffn_weight_scratch():
    return [pltpu.VMEM((D_MODEL, D_FF), BF16), pltpu.VMEM((D_MODEL, D_FF), BF16), pltpu.VMEM((D_FF, D_MODEL), BF16),
            pltpu.VMEM((COL_STAGE_DEPTH, D_MODEL, FF_COLS), F32), pltpu.SemaphoreType.DMA((COL_STAGE_DEPTH,)),
            pltpu.VMEM((ROW_STAGE_DEPTH, FF_COLS, D_MODEL), F32), pltpu.SemaphoreType.DMA((ROW_STAGE_DEPTH,))]


def _stage_a_kernel(x_ref, mods_ref, n1_ref, w1_hbm, w3_hbm, w2_hbm, nmix_ref,
                    h1_ref, a2_ref,
                    act_ref, w1_ref, w3_ref, w2_ref, col_stage, col_sem, row_stage, row_sem):
    @pl.when(pl.program_id(0) == 0)
    def _():
        _load_weights([(w1_hbm, w1_ref), (w3_hbm, w3_ref)], [(w2_hbm, w2_ref)],
                      col_stage, col_sem, row_stage, row_sem)

    @pl.when(pl.program_id(0) > 0)
    def _():
        mods = mods_ref[0]
        sh1, sc1, g1, sh2, sc2 = mods[0:1], mods[1:2], mods[2:3], mods[3:4], mods[4:5]
        for s in range(FFN_SUB_TILES):
            rows = slice(s * SUB_TILE, (s + 1) * SUB_TILE)
            x = x_ref[rows, :]
            a1 = _rmsnorm(x, n1_ref[...]) * (1.0 + sc1) + sh1
            h1 = x + (0.5 * g1) * _swiglu(a1.astype(BF16), w1_ref, w3_ref, w2_ref, act_ref.at[s])
            h1_ref[rows, :] = h1
            a2_ref[rows, :] = (_rmsnorm(h1, nmix_ref[...]) * (1.0 + sc2) + sh2).astype(BF16)


def _token_tile_specs(tile_rows, tiles_per_mod, mod_row0):
    tile_of = lambda i: jnp.maximum(i - 1, 0)
    tile = lambda width: pl.BlockSpec((tile_rows, width), lambda i: (tile_of(i), 0))
    mods = pl.BlockSpec((1, N_MOD, D_MODEL), lambda i: (mod_row0 + tile_of(i) // tiles_per_mod, 0, 0))
    return tile, mods


def _stage_a(x, mods, mod_row0, rows_per_mod, n1, w1, w3, w2, nmix, name):
    tokens, d = x.shape
    tm = SUB_TILE * FFN_SUB_TILES
    tile, mods_spec = _token_tile_specs(tm, rows_per_mod // tm, mod_row0)
    return pl.pallas_call(
        _stage_a_kernel,
        out_shape=[jax.ShapeDtypeStruct((tokens, d), F32), jax.ShapeDtypeStruct((tokens, d), BF16)],
        grid=(1 + tokens // tm,),
        in_specs=[tile(d), mods_spec, _resident((1, d)), _hbm(), _hbm(), _hbm(), _resident((1, d))],
        out_specs=[tile(d), tile(d)],
        scratch_shapes=[pltpu.VMEM((FFN_SUB_TILES, SUB_TILE, D_FF), BF16)] + _ffn_weight_scratch(),
        compiler_params=pltpu.CompilerParams(
            dimension_semantics=("arbitrary",), vmem_limit_bytes=V7X_VMEM_LIMIT),
        name=name,
    )(x, mods, n1, w1, w3, w2, nmix)


def _stage_c_kernel(h1_ref, ret_ref, pool_ref, mods_ref, wout_hbm, n2_ref, w1_hbm, w3_hbm, w2_hbm, nf_ref,
                    y_ref,
                    act_ref, w1_ref, w3_ref, w2_ref, col_stage, col_sem, row_stage, row_sem, wout_ref):
    @pl.when(pl.program_id(0) == 0)
    def _():
        _load_weights([(wout_hbm, wout_ref), (w1_hbm, w1_ref), (w3_hbm, w3_ref)], [(w2_hbm, w2_ref)],
                      col_stage, col_sem, row_stage, row_sem)

    @pl.when(pl.program_id(0) > 0)
    def _():
        mods = mods_ref[0]
        g2, sh3, sc3, g3 = mods[5:6], mods[6:7], mods[7:8], mods[8:9]
        for s in range(FFN_SUB_TILES):
            rows = slice(s * SUB_TILE, (s + 1) * SUB_TILE)
            mix = (_dot(ret_ref[rows, :], wout_ref[0:RET_WIDTH, :])
                   + _dot(pool_ref[rows, :], wout_ref[RET_WIDTH:, :]))
            h2 = h1_ref[rows, :] + g2 * mix
            a3 = _rmsnorm(h2, n2_ref[...]) * (1.0 + sc3) + sh3
            h3 = h2 + (0.5 * g3) * _swiglu(a3.astype(BF16), w1_ref, w3_ref, w2_ref, act_ref.at[s])
            y_ref[rows, :] = _rmsnorm(h3, nf_ref[...])


def _stage_c(h1, ret, pool, mods, mod_row0, rows_per_mod, w_out, n2, w1, w3, w2, nf, name):
    tokens, d = h1.shape
    tm = SUB_TILE * FFN_SUB_TILES
    tile, mods_spec = _token_tile_specs(tm, rows_per_mod // tm, mod_row0)
    return pl.pallas_call(
        _stage_c_kernel,
        out_shape=jax.ShapeDtypeStruct((tokens, d), F32),
        grid=(1 + tokens // tm,),
        in_specs=[tile(d), tile(RET_WIDTH), tile(RET_WIDTH), mods_spec,
                  _hbm(), _resident((1, d)), _hbm(), _hbm(), _hbm(), _resident((1, d))],
        out_specs=tile(d),
        scratch_shapes=([pltpu.VMEM((FFN_SUB_TILES, SUB_TILE, D_FF), BF16)] + _ffn_weight_scratch()
                        + [pltpu.VMEM(w_out.shape, BF16)]),
        compiler_params=pltpu.CompilerParams(
            dimension_semantics=("arbitrary",), vmem_limit_bytes=V7X_VMEM_LIMIT),
        name=name,
    )(h1, ret, pool, mods, w_out, n2, w1, w3, w2, nf)


def _ret_tables(dec_f_ref, dec_b_ref, dmat_ref, qd_ref, kd_ref, cd_ref):
    c = RET_CHUNK
    row = lax.broadcasted_iota(jnp.int32, (c, c), 0)
    col = lax.broadcasted_iota(jnp.int32, (c, c), 1)
    rel = (row - col).astype(F32)
    r = row[:, :HEAD_DIM].astype(F32)
    for h in range(N_HEADS):
        lg_f = -jnp.exp(dec_f_ref[h])
        lg_b = -jnp.exp(dec_b_ref[h])
        dmat_ref[h] = (jnp.where(rel >= 0, jnp.exp(lg_f * jnp.maximum(rel, 0.0)), 0.0)
                       + jnp.where(rel <= 0, jnp.exp(lg_b * jnp.maximum(-rel, 0.0)), 0.0))
        lf, lb = lg_f[:, :HEAD_DIM], lg_b[:, :HEAD_DIM]
        qd_ref[h] = jnp.concatenate([jnp.exp(lf * (r + 1.0)), jnp.exp(lb * (c - r))], axis=1)
        kd_ref[h] = jnp.concatenate([jnp.exp(lf * (c - 1.0 - r)), jnp.exp(lb * r)], axis=1)
        cd_ref[h] = jnp.concatenate([jnp.exp(lf * c), jnp.exp(lb * c)], axis=1)


def _ret_table_scratch():
    c = RET_CHUNK
    return [pltpu.VMEM((N_HEADS, c, c), F32), pltpu.VMEM((N_HEADS, c, 2 * HEAD_DIM), F32),
            pltpu.VMEM((N_HEADS, c, 2 * HEAD_DIM), F32), pltpu.VMEM((N_HEADS, 1, 2 * HEAD_DIM), F32)]


def _project(a2, win_ref, part, rot):
    t = _dot(a2, win_ref[:, part * RET_WIDTH:(part + 1) * RET_WIDTH])
    if part == 1:
        t = t * (HEAD_DIM ** -0.5)
    if part < 2 and rot is not None:
        cos2, sin2 = rot
        t = jnp.concatenate([t[:, _head(h)] * cos2 + pltpu.roll(t[:, _head(h)], HEAD_DIM // 2, axis=1) * sin2
                             for h in range(N_HEADS)], axis=-1)
    return t


def _chunk_kv(k_c, v_c, kd):
    v32 = v_c.astype(F32)
    vd = (jnp.concatenate([v32, v32], axis=1) * kd).astype(BF16)
    return _dot_tn(k_c, vd)


def _ret_out(q_c, k_c, v_c, dmat, cross, gate_c, gn):
    p = (_dot_nt(q_c, k_c) * dmat).astype(BF16)
    o = _dot(p, v_c)
    if cross is not None:
        o = o + cross[:, :HEAD_DIM] + cross[:, HEAD_DIM:]
    o = o * lax.rsqrt(jnp.mean(o * o, axis=-1, keepdims=True) + EPS) * gn
    return (o * _silu(gate_c)).astype(BF16)


def _pool_inv_count(t0, rows, seq_len, window):
    def edge(start):
        t = start + lax.broadcasted_iota(jnp.int32, (POOL_HALO, HEAD_DIM), 0)
        cnt = jnp.minimum(t + window // 2, seq_len) - jnp.maximum(t - window // 2, 0)
        return 1.0 / cnt.astype(F32)
    inner = jnp.full((rows - 2 * POOL_HALO, HEAD_DIM), 1.0 / window, F32)
    return jnp.concatenate([edge(t0), inner, edge(t0 + rows - POOL_HALO)], axis=0)


def _pool_centred(ext, t0, rows, seq_len, window):
    n_ext = rows + 2 * POOL_HALO
    acc = ext
    span = 1
    while span < window:
        acc = acc + pltpu.roll(acc, span, axis=0)
        span *= 2
    lead = window // 2 - 1
    if lead:
        acc = pltpu.roll(acc, n_ext - lead, axis=0)
    win = acc[POOL_HALO:POOL_HALO + rows]
    tok = ext[POOL_HALO:POOL_HALO + rows]
    return win * _pool_inv_count(t0, rows, seq_len, window) - tok


def _pool_groups(ext_of_group, pw_ref, ps_ref, t0, rows, seq_len, store):
    for g, window in enumerate(POOL_WINDOWS):
        centred = _pool_centred(ext_of_group(g), t0, rows, seq_len, window)
        store(g, (_dot(centred.astype(BF16), pw_ref[g].astype(BF16)) * ps_ref[:, _head(g)]).astype(BF16))


def _stage_b_context_kernel(a2_ref, win_ref, dec_f_ref, dec_b_ref, gn_ref, pw_ref, ps_ref,
                            ret_ref, pool_ref, sf_ref, sb_ref, dmat_ref, qd_ref, kd_ref, cd_ref, ext_ref,
                            *, seq_len):
    @pl.when(pl.program_id(0) == 0)
    def _():
        _ret_tables(dec_f_ref, dec_b_ref, dmat_ref, qd_ref, kd_ref, cd_ref)
        zeros = jnp.zeros((POOL_HALO, RET_WIDTH), F32)
        for s in range(ext_ref.shape[0]):
            ext_ref[s, 0:POOL_HALO, :] = zeros
            ext_ref[s, POOL_HALO + seq_len:, :] = zeros

    a2 = a2_ref[...]
    q, k, v = (_project(a2, win_ref, part, None).astype(BF16) for part in range(3))
    gate, u = _project(a2, win_ref, 3, None), _project(a2, win_ref, 4, None)
    for s in range(a2_ref.shape[0] // seq_len):
        rows = slice(s * seq_len, (s + 1) * seq_len)
        for h in range(N_HEADS):
            q_c, k_c, v_c = q[rows, _head(h)], k[rows, _head(h)], v[rows, _head(h)]
            ret_ref[rows, _head(h)] = _ret_out(q_c, k_c, v_c, dmat_ref[h], None, gate[rows, _head(h)],
                                               gn_ref[:, _head(h)])
            kv = _chunk_kv(k_c, v_c, kd_ref[h])
            sf_ref[s, 0, h] = kv[:, :HEAD_DIM]
            sb_ref[s, 0, h] = kv[:, HEAD_DIM:]
        ext_ref[s, POOL_HALO:POOL_HALO + seq_len, :] = u[rows, :]

        def store(g, val, rows=rows):
            pool_ref[rows, _head(g)] = val
        _pool_groups(lambda g, s=s: ext_ref[s, :, _head(g)], pw_ref, ps_ref, 0, seq_len, seq_len, store)


def _stage_b_context(a2, w_in, dec_f, dec_b, gn, pool_w, pool_scale, n_seq, seq_len):
    assert seq_len == RET_CHUNK
    tokens = n_seq * seq_len
    tl = RET_TILE
    seqs = tl // seq_len
    tile = lambda width: pl.BlockSpec((tl, width), lambda i: (i, 0))
    state_shape = jax.ShapeDtypeStruct((n_seq, 1, N_HEADS, HEAD_DIM, HEAD_DIM), F32)
    state_spec = lambda: pl.BlockSpec((seqs, 1, N_HEADS, HEAD_DIM, HEAD_DIM), lambda i: (i, 0, 0, 0, 0))
    return pl.pallas_call(
        functools.partial(_stage_b_context_kernel, seq_len=seq_len),
        out_shape=[jax.ShapeDtypeStruct((tokens, RET_WIDTH), BF16)] * 2 + [state_shape] * 2,
        grid=(tokens // tl,),
        in_specs=[tile(D_MODEL), _resident(w_in.shape),
                  _resident(dec_f.shape), _resident(dec_b.shape), _resident(gn.shape),
                  _resident(pool_w.shape), _resident(pool_scale.shape)],
        out_specs=[tile(RET_WIDTH), tile(RET_WIDTH), state_spec(), state_spec()],
        scratch_shapes=_ret_table_scratch() + [pltpu.VMEM((seqs, seq_len + 2 * POOL_HALO, RET_WIDTH), F32)],
        compiler_params=pltpu.CompilerParams(
            dimension_semantics=("arbitrary",), vmem_limit_bytes=V7X_VMEM_LIMIT),
        name="stage_b_context",
    )(a2, w_in, dec_f, dec_b, gn, pool_w, pool_scale)


def _stage_b_latent_kernel(a2_ref, cos_ref, sin_ref, win_ref, dec_f_ref, dec_b_ref, gn_ref, pw_ref, ps_ref,
                           s0f_ref, s0b_ref,
                           ret_ref, pool_ref,
                           dmat_ref, qd_ref, kd_ref, cd_ref, sf_ref, sb_ref, sb_hist_ref, kvf_hist_ref,
                           kseq_ref, vseq_ref, useq_ref,
                           *, seq_len, n_tiles):
    c = RET_CHUNK
    tl = a2_ref.shape[1]
    chunks = tl // c
    j = pl.program_id(1)
    rot = (cos_ref[...], sin_ref[...])

    @pl.when((pl.program_id(0) == 0) & (j == 0))
    def _():
        _ret_tables(dec_f_ref, dec_b_ref, dmat_ref, qd_ref, kd_ref, cd_ref)
        zeros = jnp.zeros((POOL_HALO, RET_WIDTH), F32)
        useq_ref[0:POOL_HALO, :] = zeros
        useq_ref[POOL_HALO + seq_len:, :] = zeros

    @pl.when(j == 0)
    def _():
        for h in range(N_HEADS):
            sb_ref[h] = s0b_ref[0, h]
            sf_ref[h] = s0f_ref[0, h]

    @pl.when(j < n_tiles)
    def _():
        tile = n_tiles - 1 - j
        t0 = pl.multiple_of(tile * tl, tl)
        a2 = a2_ref[0]
        k = _project(a2, win_ref, 1, rot).astype(BF16)
        v = _project(a2, win_ref, 2, rot).astype(BF16)
        kseq_ref[pl.ds(t0, tl), :] = k
        vseq_ref[pl.ds(t0, tl), :] = v
        useq_ref[pl.ds(t0 + POOL_HALO, tl), :] = _project(a2, win_ref, 4, rot)
        for ci in reversed(range(chunks)):
            n = tile * chunks + ci
            rows = slice(ci * c, (ci + 1) * c)
            for h in range(N_HEADS):
                kv = _chunk_kv(k[rows, _head(h)], v[rows, _head(h)], kd_ref[h])
                s_b = sb_ref[h]
                sb_hist_ref[n, h] = s_b.astype(BF16)
                kvf_hist_ref[n, h] = kv[:, :HEAD_DIM]
                sb_ref[h] = s_b * cd_ref[h][:, HEAD_DIM:] + kv[:, HEAD_DIM:]

    @pl.when(j >= n_tiles)
    def _():
        tile = j - n_tiles
        t0 = pl.multiple_of(tile * tl, tl)
        a2 = a2_ref[0]
        q = _project(a2, win_ref, 0, rot).astype(BF16)
        gate = _project(a2, win_ref, 3, rot)
        for ci in range(chunks):
            n = tile * chunks + ci
            rows = slice(ci * c, (ci + 1) * c)
            seq_rows = pl.ds(t0 + ci * c, c)
            for h in range(N_HEADS):
                q_c = q[rows, _head(h)]
                s_f = sf_ref[h]
                s2 = jnp.concatenate([s_f.astype(BF16), sb_hist_ref[n, h]], axis=1)
                cross = _dot(q_c, s2) * qd_ref[h]
                ret_ref[0, rows, _head(h)] = _ret_out(q_c, kseq_ref[seq_rows, _head(h)], vseq_ref[seq_rows, _head(h)],
                                                      dmat_ref[h], cross, gate[rows, _head(h)], gn_ref[:, _head(h)])
                sf_ref[h] = s_f * cd_ref[h][:, :HEAD_DIM] + kvf_hist_ref[n, h]

        def store(g, val):
            pool_ref[0, :, _head(g)] = val
        _pool_groups(lambda g: useq_ref[pl.ds(t0, tl + 2 * POOL_HALO), _head(g)], pw_ref, ps_ref,
                     tile * tl, tl, seq_len, store)


def _stage_b_latent(a2, rot, w_in, dec_f, dec_b, gn, pool_w, pool_scale, s0f, s0b):
    b, l, d = a2.shape
    tl = RET_TILE
    nt = l // tl
    n_chunks = l // RET_CHUNK
    both_tile = lambda j: jnp.where(j < nt, nt - 1 - j, j - nt)
    fwd = lambda: pl.BlockSpec((1, tl, RET_WIDTH), lambda i, j: (i, jnp.maximum(j - nt, 0), 0))
    rot_spec = lambda: pl.BlockSpec((tl, HEAD_DIM), lambda i, j: (both_tile(j), 0))
    state = lambda: pl.BlockSpec((1, N_HEADS, HEAD_DIM, HEAD_DIM), lambda i, j: (i, 0, 0, 0))
    return pl.pallas_call(
        functools.partial(_stage_b_latent_kernel, seq_len=l, n_tiles=nt),
        out_shape=[jax.ShapeDtypeStruct((b, l, RET_WIDTH), BF16)] * 2,
        grid=(b, 2 * nt),
        in_specs=[pl.BlockSpec((1, tl, d), lambda i, j: (i, both_tile(j), 0)), rot_spec(), rot_spec(),
                  _resident(w_in.shape), _resident(dec_f.shape), _resident(dec_b.shape), _resident(gn.shape),
                  _resident(pool_w.shape), _resident(pool_scale.shape), state(), state()],
        out_specs=[fwd(), fwd()],
        scratch_shapes=_ret_table_scratch() + [
            pltpu.VMEM((N_HEADS, HEAD_DIM, HEAD_DIM), F32), pltpu.VMEM((N_HEADS, HEAD_DIM, HEAD_DIM), F32),
            pltpu.VMEM((n_chunks, N_HEADS, HEAD_DIM, HEAD_DIM), BF16),
            pltpu.VMEM((n_chunks, N_HEADS, HEAD_DIM, HEAD_DIM), F32),
            pltpu.VMEM((l, RET_WIDTH), BF16), pltpu.VMEM((l, RET_WIDTH), BF16),
            pltpu.VMEM((l + 2 * POOL_HALO, RET_WIDTH), F32)],
        compiler_params=pltpu.CompilerParams(
            dimension_semantics=("arbitrary", "arbitrary"), vmem_limit_bytes=V7X_VMEM_LIMIT),
        name="stage_b_latent",
    )(a2, *rot, w_in, dec_f, dec_b, gn, pool_w, pool_scale, s0f, s0b)


def _rotary_tables(seq_len):
    rows = seq_len // GRID_W
    row = jnp.repeat(jnp.arange(rows, dtype=F32), GRID_W)
    col = jnp.tile(jnp.arange(GRID_W, dtype=F32), rows)
    n_half = HEAD_DIM // 4
    freqs = ROPE_BASE ** (-jnp.arange(n_half, dtype=F32) / n_half)
    ang = jnp.concatenate([row[:, None] * freqs, col[:, None] * freqs], axis=-1)
    cos, sin = jnp.cos(ang), jnp.sin(ang)
    return jnp.concatenate([cos, cos], axis=-1), jnp.concatenate([-sin, sin], axis=-1)


def kernel(x_prompt, x_sample, state_ret_fwd, state_ret_bwd, c, c_ctx, ada_w, ada_b, norm_ffn1, ffn1_w1, ffn1_w3, ffn1_w2, norm_mix, w_in, ret_decay_fwd, ret_decay_bwd, ret_gn, pool_w, pool_scale, w_out, norm_ffn2, ffn2_w1, ffn2_w3, ffn2_w2, norm_final):
    depth = ada_w.shape[0]
    assert depth == 1, "single trunk layer"
    d = D_MODEL
    n_ctx, l_ctx, _ = x_prompt.shape
    n_lat, l_lat, _ = x_sample.shape
    t_ctx, t_lat = n_ctx * l_ctx, n_lat * l_lat

    cond = jnp.concatenate([c_ctx[None, :], c, jnp.zeros((8 - 1 - n_lat, d), F32)], axis=0)
    mods = _mods(cond, ada_w[0], ada_b[0])

    row = lambda g: g.reshape(1, -1)
    ffn1 = (ffn1_w1[0], ffn1_w3[0], ffn1_w2[0])
    ffn2 = (ffn2_w1[0], ffn2_w3[0], ffn2_w2[0])
    w_in_b = w_in[0].astype(BF16)
    n1, nmix, n2, nf = row(norm_ffn1[0]), row(norm_mix[0]), row(norm_ffn2[0]), row(norm_final)
    dec_f = jnp.broadcast_to(ret_decay_fwd[0][:, None, None], (N_HEADS, 1, RET_CHUNK))
    dec_b = jnp.broadcast_to(ret_decay_bwd[0][:, None, None], (N_HEADS, 1, RET_CHUNK))
    gn, ps = row(ret_gn[0]), row(pool_scale[0])

    h1c, a2c = _stage_a(x_prompt.reshape(t_ctx, d), mods, 0, t_ctx, n1, *ffn1, nmix, "stage_a_context")
    retc, poolc, new_f, new_b = _stage_b_context(a2c, w_in_b, dec_f, dec_b, gn, pool_w[0], ps, n_ctx, l_ctx)
    y_prompt = _stage_c(h1c, retc, poolc, mods, 0, t_ctx, w_out[0], n2, *ffn2, nf, "stage_c_context")

    h1l, a2l = _stage_a(x_sample.reshape(t_lat, d), mods, 1, l_lat, n1, *ffn1, nmix, "stage_a_latent")
    retl, pooll = _stage_b_latent(a2l.reshape(n_lat, l_lat, d), _rotary_tables(l_lat), w_in_b, dec_f, dec_b, gn,
                                  pool_w[0], ps, state_ret_fwd[:, 0], state_ret_bwd[:, 0])
    y_sample = _stage_c(h1l, retl.reshape(t_lat, RET_WIDTH), pooll.reshape(t_lat, RET_WIDTH), mods, 1, l_lat,
                        w_out[0], n2, *ffn2, nf, "stage_c_latent")

    return (y_prompt.reshape(n_ctx, l_ctx, d), y_sample.reshape(n_lat, l_lat, d), new_f, new_b)
```

```python
import functools

import jax
import jax.numpy as jnp
import numpy as np
from jax import lax
from jax.experimental import pallas as pl
from jax.experimental.pallas import tpu as pltpu

D_MODEL = 1024
GRID_W = 64
N_HEADS = 4
HEAD_DIM = 128
RET_WIDTH = N_HEADS * HEAD_DIM
POOL_WINDOWS = (2, 4, 8, 16)
POOL_HALO = 8
D_FF = 2816
ROPE_BASE = 10000.0
N_MOD = 9
EPS = 1e-6

FF_COLS = 256
SUB_TILE = 512
FFN_SUB_TILES = 2
RET_TILE = 1024
RET_CHUNK = 256
COL_STAGE_DEPTH = 4
ROW_STAGE_DEPTH = 2
PANEL_SPLIT = 4
V7X_VMEM_LIMIT = 56 * 1024 * 1024

F32 = jnp.float32
BF16 = jnp.bfloat16


def _silu(x):
    return x * (1.0 / (1.0 + jnp.exp(-x)))


def _rmsnorm(x, g):
    return x * lax.rsqrt(jnp.mean(x * x, axis=-1, keepdims=True) + EPS) * g


def _dot(a, b):
    return jnp.dot(a, b, preferred_element_type=F32)


def _dot_tn(a, b):
    return lax.dot_general(a, b, (((0,), (0,)), ((), ())), preferred_element_type=F32)


def _dot_nt(a, b):
    return lax.dot_general(a, b, (((1,), (1,)), ((), ())), preferred_element_type=F32)


def _head(h):
    return slice(h * HEAD_DIM, (h + 1) * HEAD_DIM)


def _resident(shape):
    nd = len(shape)
    return pl.BlockSpec(shape, lambda *_: (0,) * nd, pipeline_mode=pl.Buffered(1))


def _hbm():
    return pl.BlockSpec(memory_space=pl.ANY)


def _mods_kernel(cond_ref, w_ref, b_ref, o_ref):
    s = _silu(cond_ref[...]).astype(BF16)
    o_ref[...] = _dot(s, w_ref[...].astype(BF16)) + b_ref[...]


def _mods(cond, ada_w, ada_b):
    n = N_MOD * D_MODEL
    out = pl.pallas_call(
        _mods_kernel,
        out_shape=jax.ShapeDtypeStruct((8, n), F32),
        grid=(N_MOD,),
        in_specs=[
            pl.BlockSpec((8, D_MODEL), lambda j: (0, 0)),
            pl.BlockSpec((D_MODEL, D_MODEL), lambda j: (0, j)),
            pl.BlockSpec((1, D_MODEL), lambda j: (0, j)),
        ],
        out_specs=pl.BlockSpec((8, D_MODEL), lambda j: (0, j)),
        name="adaln_mods",
    )(cond, ada_w, ada_b.reshape(1, n))
    return out.reshape(8, N_MOD, D_MODEL)


def _swiglu(a_bf16, w1_ref, w3_ref, w2_ref, act_ref):
    for c in range(D_FF // FF_COLS):
        cols = slice(c * FF_COLS, (c + 1) * FF_COLS)
        g = _dot(a_bf16, w1_ref[:, cols])
        u = _dot(a_bf16, w3_ref[:, cols])
        act_ref[:, cols] = (_silu(g) * u).astype(BF16)
    return _dot(act_ref[...], w2_ref[...])


def _panel_stream(srcs, dsts, stage_ref, sem_ref):
    depth, rows = stage_ref.shape[0], stage_ref.shape[1]
    part = rows // PANEL_SPLIT

    def copies(k):
        slot = k % depth
        return [pltpu.make_async_copy(srcs[k].at[pl.ds(s * part, part), :],
                                      stage_ref.at[slot, pl.ds(s * part, part), :], sem_ref.at[slot])
                for s in range(PANEL_SPLIT)]

    def start(k):
        for cp in copies(k):
            cp.start()

    def prime():
        for k in range(min(depth, len(srcs))):
            start(k)

    def take(k):
        for cp in copies(k):
            cp.wait()
        dsts[k][...] = stage_ref[k % depth].astype(BF16)
        if k + depth < len(srcs):
            start(k + depth)

    return prime, take


def _col_panels(ref):
    return [ref.at[:, pl.ds(p * FF_COLS, FF_COLS)] for p in range(ref.shape[1] // FF_COLS)]


def _row_panels(ref):
    return [ref.at[pl.ds(p * FF_COLS, FF_COLS), :] for p in range(ref.shape[0] // FF_COLS)]


def _load_weights(col_pairs, row_pairs, col_stage, col_sem, row_stage, row_sem):
    col_src = [p for src, _ in col_pairs for p in _col_panels(src)]
    col_dst = [p for _, dst in col_pairs for p in _col_panels(dst)]
    row_src = [p for src, _ in row_pairs for p in _row_panels(src)]
    row_dst = [p for _, dst in row_pairs for p in _row_panels(dst)]
    col_prime, col_take = _panel_stream(col_src, col_dst, col_stage, col_sem)
    row_prime, row_take = _panel_stream(row_src, row_dst, row_stage, row_sem)
    col_prime()
    row_prime()
    n_col, n_row = len(col_src), len(row_src)
    for k in range(max(n_col, n_row)):
        for kc in range(k * n_col // max(n_col, n_row), (k + 1) * n_col // max(n_col, n_row)):
            col_take(kc)
        for kr in range(k * n_row // max(n_col, n_row), (k + 1) * n_row // max(n_col, n_row)):
            row_take(kr)


def _ffn_weight_scratch():
    return [pltpu.VMEM((D_MODEL, D_FF), BF16), pltpu.VMEM((D_MODEL, D_FF), BF16), pltpu.VMEM((D_FF, D_MODEL), BF16),
            pltpu.VMEM((COL_STAGE_DEPTH, D_MODEL, FF_COLS), F32), pltpu.SemaphoreType.DMA((COL_STAGE_DEPTH,)),
            pltpu.VMEM((ROW_STAGE_DEPTH, FF_COLS, D_MODEL), F32), pltpu.SemaphoreType.DMA((ROW_STAGE_DEPTH,))]


def _stage_a_kernel(x_ref, mods_ref, n1_ref, w1_hbm, w3_hbm, w2_hbm, nmix_ref,
                    h1_ref, a2_ref,
                    act_ref, w1_ref, w3_ref, w2_ref, col_stage, col_sem, row_stage, row_sem):
    @pl.when(pl.program_id(0) == 0)
    def _():
        _load_weights([(w1_hbm, w1_ref), (w3_hbm, w3_ref)], [(w2_hbm, w2_ref)],
                      col_stage, col_sem, row_stage, row_sem)

    @pl.when(pl.program_id(0) > 0)
    def _():
        mods = mods_ref[0]
        sh1, sc1, g1, sh2, sc2 = mods[0:1], mods[1:2], mods[2:3], mods[3:4], mods[4:5]
        for s in range(FFN_SUB_TILES):
            rows = slice(s * SUB_TILE, (s + 1) * SUB_TILE)
            x = x_ref[rows, :]
            a1 = _rmsnorm(x, n1_ref[...]) * (1.0 + sc1) + sh1
            h1 = x + (0.5 * g1) * _swiglu(a1.astype(BF16), w1_ref, w3_ref, w2_ref, act_ref.at[s])
            h1_ref[rows, :] = h1
            a2_ref[rows, :] = (_rmsnorm(h1, nmix_ref[...]) * (1.0 + sc2) + sh2).astype(BF16)


def _token_tile_specs(tile_rows, tiles_per_mod, mod_row0):
    tile_of = lambda i: jnp.maximum(i - 1, 0)
    tile = lambda width: pl.BlockSpec((tile_rows, width), lambda i: (tile_of(i), 0))
    mods = pl.BlockSpec((1, N_MOD, D_MODEL), lambda i: (mod_row0 + tile_of(i) // tiles_per_mod, 0, 0))
    return tile, mods


def _stage_a(x, mods, mod_row0, rows_per_mod, n1, w1, w3, w2, nmix, name):
    tokens, d = x.shape
    tm = SUB_TILE * FFN_SUB_TILES
    tile, mods_spec = _token_tile_specs(tm, rows_per_mod // tm, mod_row0)
    return pl.pallas_call(
        _stage_a_kernel,
        out_shape=[jax.ShapeDtypeStruct((tokens, d), F32), jax.ShapeDtypeStruct((tokens, d), BF16)],
        grid=(1 + tokens // tm,),
        in_specs=[tile(d), mods_spec, _resident((1, d)), _hbm(), _hbm(), _hbm(), _resident((1, d))],
        out_specs=[tile(d), tile(d)],
        scratch_shapes=[pltpu.VMEM((FFN_SUB_TILES, SUB_TILE, D_FF), BF16)] + _ffn_weight_scratch(),
        compiler_params=pltpu.CompilerParams(
            dimension_semantics=("arbitrary",), vmem_limit_bytes=V7X_VMEM_LIMIT),
        name=name,
    )(x, mods, n1, w1, w3, w2, nmix)


def _stage_c_kernel(h1_ref, ret_ref, pool_ref, mods_ref, wout_hbm, n2_ref, w1_hbm, w3_hbm, w2_hbm, nf_ref,
                    y_ref,
                    act_ref, w1_ref, w3_ref, w2_ref, col_stage, col_sem, row_stage, row_sem, wout_ref):
    @pl.when(pl.program_id(0) == 0)
    def _():
        _load_weights([(wout_hbm, wout_ref), (w1_hbm, w1_ref), (w3_hbm, w3_ref)], [(w2_hbm, w2_ref)],
                      col_stage, col_sem, row_stage, row_sem)

    @pl.when(pl.program_id(0) > 0)
    def _():
        mods = mods_ref[0]
        g2, sh3, sc3, g3 = mods[5:6], mods[6:7], mods[7:8], mods[8:9]
        for s in range(FFN_SUB_TILES):
            rows = slice(s * SUB_TILE, (s + 1) * SUB_TILE)
            mix = (_dot(ret_ref[rows, :], wout_ref[0:RET_WIDTH, :])
                   + _dot(pool_ref[rows, :], wout_ref[RET_WIDTH:, :]))
            h2 = h1_ref[rows, :] + g2 * mix
            a3 = _rmsnorm(h2, n2_ref[...]) * (1.0 + sc3) + sh3
            h3 = h2 + (0.5 * g3) * _swiglu(a3.astype(BF16), w1_ref, w3_ref, w2_ref, act_ref.at[s])
            y_ref[rows, :] = _rmsnorm(h3, nf_ref[...])


def _stage_c(h1, ret, pool, mods, mod_row0, rows_per_mod, w_out, n2, w1, w3, w2, nf, name):
    tokens, d = h1.shape
    tm = SUB_TILE * FFN_SUB_TILES
    tile, mods_spec = _token_tile_specs(tm, rows_per_mod // tm, mod_row0)
    return pl.pallas_call(
        _stage_c_kernel,
        out_shape=jax.ShapeDtypeStruct((tokens, d), F32),
        grid=(1 + tokens // tm,),
        in_specs=[tile(d), tile(RET_WIDTH), tile(RET_WIDTH), mods_spec,
                  _hbm(), _resident((1, d)), _hbm(), _hbm(), _hbm(), _resident((1, d))],
        out_specs=tile(d),
        scratch_shapes=([pltpu.VMEM((FFN_SUB_TILES, SUB_TILE, D_FF), BF16)] + _ffn_weight_scratch()
                        + [pltpu.VMEM(w_out.shape, BF16)]),
        compiler_params=pltpu.CompilerParams(
            dimension_semantics=("arbitrary",), vmem_limit_bytes=V7X_VMEM_LIMIT),
        name=name,
    )(h1, ret, pool, mods, w_out, n2, w1, w3, w2, nf)


def _ret_tables(dec_f_ref, dec_b_ref, dmat_ref, qd_ref, kd_ref, cd_ref):
    c = RET_CHUNK
    row = lax.broadcasted_iota(jnp.int32, (c, c), 0)
    col = lax.broadcasted_iota(jnp.int32, (c, c), 1)
    rel = (row - col).astype(F32)
    r = row[:, :HEAD_DIM].astype(F32)
    for h in range(N_HEADS):
        lg_f = -jnp.exp(dec_f_ref[h])
        lg_b = -jnp.exp(dec_b_ref[h])
        dmat_ref[h] = (jnp.where(rel >= 0, jnp.exp(lg_f * jnp.maximum(rel, 0.0)), 0.0)
                       + jnp.where(rel <= 0, jnp.exp(lg_b * jnp.maximum(-rel, 0.0)), 0.0))
        lf, lb = lg_f[:, :HEAD_DIM], lg_b[:, :HEAD_DIM]
        qd_ref[h] = jnp.concatenate([jnp.exp(lf * (r + 1.0)), jnp.exp(lb * (c - r))], axis=1)
        kd_ref[h] = jnp.concatenate([jnp.exp(lf * (c - 1.0 - r)), jnp.exp(lb * r)], axis=1)
        cd_ref[h] = jnp.concatenate([jnp.exp(lf * c), jnp.exp(lb * c)], axis=1)


def _ret_table_scratch():
    c = RET_CHUNK
    return [pltpu.VMEM((N_HEADS, c, c), F32), pltpu.VMEM((N_HEADS, c, 2 * HEAD_DIM), F32),
            pltpu.VMEM((N_HEADS, c, 2 * HEAD_DIM), F32), pltpu.VMEM((N_HEADS, 1, 2 * HEAD_DIM), F32)]


def _project(a2, win_ref, part, rot):
    t = _dot(a2, win_ref[:, part * RET_WIDTH:(part + 1) * RET_WIDTH])
    if part == 1:
        t = t * (HEAD_DIM ** -0.5)
    if part < 2 and rot is not None:
        cos2, sin2 = rot
        t = jnp.concatenate([t[:, _head(h)] * cos2 + pltpu.roll(t[:, _head(h)], HEAD_DIM // 2, axis=1) * sin2
                             for h in range(N_HEADS)], axis=-1)
    return t


def _chunk_kv(k_c, v_c, kd):
    v32 = v_c.astype(F32)
    vd = (jnp.concatenate([v32, v32], axis=1) * kd).astype(BF16)
    return _dot_tn(k_c, vd)


def _ret_out(q_c, k_c, v_c, dmat, cross, gate_c, gn):
    p = (_dot_nt(q_c, k_c) * dmat).astype(BF16)
    o = _dot(p, v_c)
    if cross is not None:
        o = o + cross[:, :HEAD_DIM] + cross[:, HEAD_DIM:]
    o = o * lax.rsqrt(jnp.mean(o * o, axis=-1, keepdims=True) + EPS) * gn
    return (o * _silu(gate_c)).astype(BF16)


def _pool_inv_count(t0, rows, seq_len, window):
    def edge(start):
        t = start + lax.broadcasted_iota(jnp.int32, (POOL_HALO, HEAD_DIM), 0)
        cnt = jnp.minimum(t + window // 2, seq_len) - jnp.maximum(t - window // 2, 0)
        return 1.0 / cnt.astype(F32)
    inner = jnp.full((rows - 2 * POOL_HALO, HEAD_DIM), 1.0 / window, F32)
    return jnp.concatenate([edge(t0), inner, edge(t0 + rows - POOL_HALO)], axis=0)


def _pool_centred(ext, t0, rows, seq_len, window):
    n_ext = rows + 2 * POOL_HALO
    half = window // 2
    acc, span = ext, 1
    while span < half:
        acc = acc + pltpu.roll(acc, n_ext - span, axis=0)
        span *= 2
    acc = acc + pltpu.roll(acc, half, axis=0)
    win = acc[POOL_HALO:POOL_HALO + rows]
    tok = ext[POOL_HALO:POOL_HALO + rows]
    return win * _pool_inv_count(t0, rows, seq_len, window) - tok


def _pool_groups(ext_of_group, pw_ref, ps_ref, t0, rows, seq_len, store):
    for g, window in enumerate(POOL_WINDOWS):
        centred = _pool_centred(ext_of_group(g), t0, rows, seq_len, window)
        store(g, (_dot(centred.astype(BF16), pw_ref[g].astype(BF16)) * ps_ref[:, _head(g)]).astype(BF16))


def _stage_b_context_kernel(a2_ref, win_ref, dec_f_ref, dec_b_ref, gn_ref, pw_ref, ps_ref,
                            ret_ref, pool_ref, sf_ref, sb_ref, dmat_ref, qd_ref, kd_ref, cd_ref, ext_ref,
                            *, seq_len):
    @pl.when(pl.program_id(0) == 0)
    def _():
        _ret_tables(dec_f_ref, dec_b_ref, dmat_ref, qd_ref, kd_ref, cd_ref)
        zeros = jnp.zeros((POOL_HALO, RET_WIDTH), F32)
        for s in range(ext_ref.shape[0]):
            ext_ref[s, 0:POOL_HALO, :] = zeros
            ext_ref[s, POOL_HALO + seq_len:, :] = zeros

    a2 = a2_ref[...]
    q, k, v = (_project(a2, win_ref, part, None).astype(BF16) for part in range(3))
    gate, u = _project(a2, win_ref, 3, None), _project(a2, win_ref, 4, None)
    for s in range(a2_ref.shape[0] // seq_len):
        rows = slice(s * seq_len, (s + 1) * seq_len)
        for h in range(N_HEADS):
            q_c, k_c, v_c = q[rows, _head(h)], k[rows, _head(h)], v[rows, _head(h)]
            ret_ref[rows, _head(h)] = _ret_out(q_c, k_c, v_c, dmat_ref[h], None, gate[rows, _head(h)],
                                               gn_ref[:, _head(h)])
            kv = _chunk_kv(k_c, v_c, kd_ref[h])
            sf_ref[s, 0, h] = kv[:, :HEAD_DIM]
            sb_ref[s, 0, h] = kv[:, HEAD_DIM:]
        ext_ref[s, POOL_HALO:POOL_HALO + seq_len, :] = u[rows, :]

        def store(g, val, rows=rows):
            pool_ref[rows, _head(g)] = val
        _pool_groups(lambda g, s=s: ext_ref[s, :, _head(g)], pw_ref, ps_ref, 0, seq_len, seq_len, store)


def _stage_b_context(a2, w_in, dec_f, dec_b, gn, pool_w, pool_scale, n_seq, seq_len):
    assert seq_len == RET_CHUNK
    tokens = n_seq * seq_len
    tl = RET_TILE
    seqs = tl // seq_len
    tile = lambda width: pl.BlockSpec((tl, width), lambda i: (i, 0))
    state_shape = jax.ShapeDtypeStruct((n_seq, 1, N_HEADS, HEAD_DIM, HEAD_DIM), F32)
    state_spec = lambda: pl.BlockSpec((seqs, 1, N_HEADS, HEAD_DIM, HEAD_DIM), lambda i: (i, 0, 0, 0, 0))
    return pl.pallas_call(
        functools.partial(_stage_b_context_kernel, seq_len=seq_len),
        out_shape=[jax.ShapeDtypeStruct((tokens, RET_WIDTH), BF16)] * 2 + [state_shape] * 2,
        grid=(tokens // tl,),
        in_specs=[tile(D_MODEL), _resident(w_in.shape),
                  _resident(dec_f.shape), _resident(dec_b.shape), _resident(gn.shape),
                  _resident(pool_w.shape), _resident(pool_scale.shape)],
        out_specs=[tile(RET_WIDTH), tile(RET_WIDTH), state_spec(), state_spec()],
        scratch_shapes=_ret_table_scratch() + [pltpu.VMEM((seqs, seq_len + 2 * POOL_HALO, RET_WIDTH), F32)],
        compiler_params=pltpu.CompilerParams(
            dimension_semantics=("arbitrary",), vmem_limit_bytes=V7X_VMEM_LIMIT),
        name="stage_b_context",
    )(a2, w_in, dec_f, dec_b, gn, pool_w, pool_scale)


def _stage_b_latent_kernel(a2_ref, cos_ref, sin_ref, win_ref, dec_f_ref, dec_b_ref, gn_ref, pw_ref, ps_ref,
                           s0f_ref, s0b_ref,
                           ret_ref, pool_ref,
                           dmat_ref, qd_ref, kd_ref, cd_ref, sf_ref, sb_ref, sb_hist_ref, kvf_hist_ref,
                           kseq_ref, vseq_ref, useq_ref,
                           *, seq_len, n_tiles):
    c = RET_CHUNK
    tl = a2_ref.shape[1]
    chunks = tl // c
    j = pl.program_id(1)
    rot = (cos_ref[...], sin_ref[...])

    @pl.when((pl.program_id(0) == 0) & (j == 0))
    def _():
        _ret_tables(dec_f_ref, dec_b_ref, dmat_ref, qd_ref, kd_ref, cd_ref)
        zeros = jnp.zeros((POOL_HALO, RET_WIDTH), F32)
        useq_ref[0:POOL_HALO, :] = zeros
        useq_ref[POOL_HALO + seq_len:, :] = zeros

    @pl.when(j == 0)
    def _():
        for h in range(N_HEADS):
            sb_ref[h] = s0b_ref[0, h]
            sf_ref[h] = s0f_ref[0, h]

    @pl.when(j < n_tiles)
    def _():
        tile = n_tiles - 1 - j
        t0 = pl.multiple_of(tile * tl, tl)
        a2 = a2_ref[0]
        k = _project(a2, win_ref, 1, rot).astype(BF16)
        v = _project(a2, win_ref, 2, rot).astype(BF16)
        kseq_ref[pl.ds(t0, tl), :] = k
        vseq_ref[pl.ds(t0, tl), :] = v
        useq_ref[pl.ds(t0 + POOL_HALO, tl), :] = _project(a2, win_ref, 4, rot)
        for ci in reversed(range(chunks)):
            n = tile * chunks + ci
            rows = slice(ci * c, (ci + 1) * c)
            for h in range(N_HEADS):
                kv = _chunk_kv(k[rows, _head(h)], v[rows, _head(h)], kd_ref[h])
                s_b = sb_ref[h]
                sb_hist_ref[n, h] = s_b.astype(BF16)
                kvf_hist_ref[n, h] = kv[:, :HEAD_DIM]
                sb_ref[h] = s_b * cd_ref[h][:, HEAD_DIM:] + kv[:, HEAD_DIM:]

    @pl.when(j >= n_tiles)
    def _():
        tile = j - n_tiles
        t0 = pl.multiple_of(tile * tl, tl)
        a2 = a2_ref[0]
        q = _project(a2, win_ref, 0, rot).astype(BF16)
        gate = _project(a2, win_ref, 3, rot)
        for ci in range(chunks):
            n = tile * chunks + ci
            rows = slice(ci * c, (ci + 1) * c)
            seq_rows = pl.ds(t0 + ci * c, c)
            for h in range(N_HEADS):
                q_c = q[rows, _head(h)]
                s_f = sf_ref[h]
                s2 = jnp.concatenate([s_f.astype(BF16), sb_hist_ref[n, h]], axis=1)
                cross = _dot(q_c, s2) * qd_ref[h]
                ret_ref[0, rows, _head(h)] = _ret_out(q_c, kseq_ref[seq_rows, _head(h)], vseq_ref[seq_rows, _head(h)],
                                                      dmat_ref[h], cross, gate[rows, _head(h)], gn_ref[:, _head(h)])
                sf_ref[h] = s_f * cd_ref[h][:, :HEAD_DIM] + kvf_hist_ref[n, h]

        def store(g, val):
            pool_ref[0, :, _head(g)] = val
        _pool_groups(lambda g: useq_ref[pl.ds(t0, tl + 2 * POOL_HALO), _head(g)], pw_ref, ps_ref,
                     tile * tl, tl, seq_len, store)


def _stage_b_latent(a2, rot, w_in, dec_f, dec_b, gn, pool_w, pool_scale, s0f, s0b):
    b, l, d = a2.shape
    tl = RET_TILE
    nt = l // tl
    n_chunks = l // RET_CHUNK
    both_tile = lambda j: jnp.where(j < nt, nt - 1 - j, j - nt)
    fwd = lambda: pl.BlockSpec((1, tl, RET_WIDTH), lambda i, j: (i, jnp.maximum(j - nt, 0), 0))
    rot_spec = lambda: pl.BlockSpec((tl, HEAD_DIM), lambda i, j: (both_tile(j), 0))
    state = lambda: pl.BlockSpec((1, N_HEADS, HEAD_DIM, HEAD_DIM), lambda i, j: (i, 0, 0, 0))
    return pl.pallas_call(
        functools.partial(_stage_b_latent_kernel, seq_len=l, n_tiles=nt),
        out_shape=[jax.ShapeDtypeStruct((b, l, RET_WIDTH), BF16)] * 2,
        grid=(b, 2 * nt),
        in_specs=[pl.BlockSpec((1, tl, d), lambda i, j: (i, both_tile(j), 0)), rot_spec(), rot_spec(),
                  _resident(w_in.shape), _resident(dec_f.shape), _resident(dec_b.shape), _resident(gn.shape),
                  _resident(pool_w.shape), _resident(pool_scale.shape), state(), state()],
        out_specs=[fwd(), fwd()],
        scratch_shapes=_ret_table_scratch() + [
            pltpu.VMEM((N_HEADS, HEAD_DIM, HEAD_DIM), F32), pltpu.VMEM((N_HEADS, HEAD_DIM, HEAD_DIM), F32),
            pltpu.VMEM((n_chunks, N_HEADS, HEAD_DIM, HEAD_DIM), BF16),
            pltpu.VMEM((n_chunks, N_HEADS, HEAD_DIM, HEAD_DIM), F32),
            pltpu.VMEM((l, RET_WIDTH), BF16), pltpu.VMEM((l, RET_WIDTH), BF16),
            pltpu.VMEM((l + 2 * POOL_HALO, RET_WIDTH), F32)],
        compiler_params=pltpu.CompilerParams(
            dimension_semantics=("arbitrary", "arbitrary"), vmem_limit_bytes=V7X_VMEM_LIMIT),
        name="stage_b_latent",
    )(a2, *rot, w_in, dec_f, dec_b, gn, pool_w, pool_scale, s0f, s0b)


def _rotary_tables(seq_len):
    rows = seq_len // GRID_W
    row = np.repeat(np.arange(rows, dtype=np.float64), GRID_W)
    col = np.tile(np.arange(GRID_W, dtype=np.float64), rows)
    n_half = HEAD_DIM // 4
    freqs = ROPE_BASE ** (-np.arange(n_half, dtype=np.float64) / n_half)
    ang = np.concatenate([row[:, None] * freqs, col[:, None] * freqs], axis=-1)
    cos, sin = np.cos(ang), np.sin(ang)
    cos2 = np.concatenate([cos, cos], axis=-1).astype(np.float32)
    sin2 = np.concatenate([-sin, sin], axis=-1).astype(np.float32)
    return jnp.asarray(cos2), jnp.asarray(sin2)


def kernel(x_prompt, x_sample, state_ret_fwd, state_ret_bwd, c, c_ctx, ada_w, ada_b, norm_ffn1, ffn1_w1, ffn1_w3, ffn1_w2, norm_mix, w_in, ret_decay_fwd, ret_decay_bwd, ret_gn, pool_w, pool_scale, w_out, norm_ffn2, ffn2_w1, ffn2_w3, ffn2_w2, norm_final):
    depth = ada_w.shape[0]
    assert depth == 1, "single trunk layer"
    d = D_MODEL
    n_ctx, l_ctx, _ = x_prompt.shape
    n_lat, l_lat, _ = x_sample.shape
    t_ctx, t_lat = n_ctx * l_ctx, n_lat * l_lat

    cond = jnp.concatenate([c_ctx[None, :], c, jnp.zeros((8 - 1 - n_lat, d), F32)], axis=0)
    mods = _mods(cond, ada_w[0], ada_b[0])

    row = lambda g: g.reshape(1, -1)
    ffn1 = (ffn1_w1[0], ffn1_w3[0], ffn1_w2[0])
    ffn2 = (ffn2_w1[0], ffn2_w3[0], ffn2_w2[0])
    w_in_b = w_in[0].astype(BF16)
    n1, nmix, n2, nf = row(norm_ffn1[0]), row(norm_mix[0]), row(norm_ffn2[0]), row(norm_final)
    dec_f = jnp.broadcast_to(ret_decay_fwd[0][:, None, None], (N_HEADS, 1, RET_CHUNK))
    dec_b = jnp.broadcast_to(ret_decay_bwd[0][:, None, None], (N_HEADS, 1, RET_CHUNK))
    gn, ps = row(ret_gn[0]), row(pool_scale[0])

    h1c, a2c = _stage_a(x_prompt.reshape(t_ctx, d), mods, 0, t_ctx, n1, *ffn1, nmix, "stage_a_context")
    retc, poolc, new_f, new_b = _stage_b_context(a2c, w_in_b, dec_f, dec_b, gn, pool_w[0], ps, n_ctx, l_ctx)
    y_prompt = _stage_c(h1c, retc, poolc, mods, 0, t_ctx, w_out[0], n2, *ffn2, nf, "stage_c_context")

    h1l, a2l = _stage_a(x_sample.reshape(t_lat, d), mods, 1, l_lat, n1, *ffn1, nmix, "stage_a_latent")
    retl, pooll = _stage_b_latent(a2l.reshape(n_lat, l_lat, d), _rotary_tables(l_lat), w_in_b, dec_f, dec_b, gn,
                                  pool_w[0], ps, state_ret_fwd[:, 0], state_ret_bwd[:, 0])
    y_sample = _stage_c(h1l, retl.reshape(t_lat, RET_WIDTH), pooll.reshape(t_lat, RET_WIDTH), mods, 1, l_lat,
                        w_out[0], n2, *ffn2, nf, "stage_c_latent")

    return (y_prompt.reshape(n_ctx, l_ctx, d), y_sample.reshape(n_lat, l_lat, d), new_f, new_b)
```

```python
import functools

import jax
import jax.numpy as jnp
import numpy as np
from jax import lax
from jax.experimental import pallas as pl
from jax.experimental.pallas import tpu as pltpu

D_MODEL = 1024
GRID_W = 64
N_HEADS = 4
HEAD_DIM = 128
RET_WIDTH = N_HEADS * HEAD_DIM
POOL_WINDOWS = (2, 4, 8, 16)
POOL_HALO = 8
D_FF = 2816
ROPE_BASE = 10000.0
N_MOD = 9
EPS = 1e-6

FF_COLS = 256
SUB_TILES = (256, 512, 256)
TOKEN_TILE = sum(SUB_TILES)
PIECE_ROWS = 64
RET_TILE = 1024
RET_CHUNK = 256
COL_STAGE_DEPTH = 4
ROW_STAGE_DEPTH = 2
PANEL_SPLIT = 4
V7X_VMEM_LIMIT = 56 * 1024 * 1024

F32 = jnp.float32
BF16 = jnp.bfloat16


def _silu(x):
    return x * (1.0 / (1.0 + jnp.exp(-x)))


def _rmsnorm(x, g):
    return x * lax.rsqrt(jnp.mean(x * x, axis=-1, keepdims=True) + EPS) * g


def _dot(a, b):
    return jnp.dot(a, b, preferred_element_type=F32)


def _dot_tn(a, b):
    return lax.dot_general(a, b, (((0,), (0,)), ((), ())), preferred_element_type=F32)


def _dot_nt(a, b):
    return lax.dot_general(a, b, (((1,), (1,)), ((), ())), preferred_element_type=F32)


def _head(h):
    return slice(h * HEAD_DIM, (h + 1) * HEAD_DIM)


def _resident(shape):
    nd = len(shape)
    return pl.BlockSpec(shape, lambda *_: (0,) * nd, pipeline_mode=pl.Buffered(1))


def _hbm():
    return pl.BlockSpec(memory_space=pl.ANY)


def _mods_kernel(cond_ref, w_ref, b_ref, o_ref):
    s = _silu(cond_ref[...]).astype(BF16)
    o_ref[...] = _dot(s, w_ref[...].astype(BF16)) + b_ref[...]


def _mods(cond, ada_w, ada_b):
    n = N_MOD * D_MODEL
    out = pl.pallas_call(
        _mods_kernel,
        out_shape=jax.ShapeDtypeStruct((8, n), F32),
        grid=(N_MOD,),
        in_specs=[
            pl.BlockSpec((8, D_MODEL), lambda j: (0, 0)),
            pl.BlockSpec((D_MODEL, D_MODEL), lambda j: (0, j)),
            pl.BlockSpec((1, D_MODEL), lambda j: (0, j)),
        ],
        out_specs=pl.BlockSpec((8, D_MODEL), lambda j: (0, j)),
        name="adaln_mods",
    )(cond, ada_w, ada_b.reshape(1, n))
    return out.reshape(8, N_MOD, D_MODEL)


def _zero_once_computed(x):
    rows, width = x.shape
    fold = x[:, 0:HEAD_DIM]
    for j in range(1, width // HEAD_DIM):
        fold = fold + x[:, j * HEAD_DIM:(j + 1) * HEAD_DIM]
    tile = fold[0:16]
    for i in range(1, rows // 16):
        tile = tile + fold[i * 16:(i + 1) * 16]
    bits = pltpu.bitcast(tile.astype(F32), jnp.uint32)
    sixteen = jnp.uint32(16)
    bits = lax.shift_right_logical(lax.shift_right_logical(bits, sixteen), sixteen)
    return pltpu.bitcast(bits, F32).astype(BF16)


def _swiglu(a_ref, w1_ref, w3_ref, w2_ref, act_ref, side_work=None):
    for c in range(D_FF // FF_COLS):
        zero = side_work(c) if side_work is not None else None
        if zero is not None:
            a_ref[0:16, 0:HEAD_DIM] = a_ref[0:16, 0:HEAD_DIM] + zero
        a = a_ref[...]
        cols = slice(c * FF_COLS, (c + 1) * FF_COLS)
        g = _dot(a, w1_ref[:, cols])
        u = _dot(a, w3_ref[:, cols])
        act_ref[:, cols] = (_silu(g) * u).astype(BF16)
    return _dot(act_ref[...], w2_ref[...])


def _swiglu_sub_tiles(prologue_begin, prologue_rows, epilogue, a_ref, w1_ref, w3_ref, w2_ref, act_ref):
    starts = [sum(SUB_TILES[:s]) for s in range(len(SUB_TILES))]
    a_ref[0:SUB_TILES[0], :] = prologue_rows(prologue_begin(0, SUB_TILES[0]), 0, 0, SUB_TILES[0])
    for s, (r0, n) in enumerate(zip(starts, SUB_TILES)):
        side_work = None
        if s + 1 < len(SUB_TILES):
            ctx = {}

            def side_work(c, r1=starts[s + 1], n1=SUB_TILES[s + 1], ctx=ctx):
                if c == 0:
                    ctx["begin"] = prologue_begin(r1, n1)
                    return None
                off = (c - 1) * PIECE_ROWS
                if off >= n1:
                    return None
                val = prologue_rows(ctx["begin"], r1, off, PIECE_ROWS)
                a_ref[r1 + off:r1 + off + PIECE_ROWS, :] = val
                return _zero_once_computed(val)
        rows = pl.ds(r0, n)
        epilogue(r0, n, _swiglu(a_ref.at[rows, :], w1_ref, w3_ref, w2_ref, act_ref.at[rows, :], side_work))


def _panel_stream(srcs, dsts, stage_ref, sem_ref):
    depth, rows = stage_ref.shape[0], stage_ref.shape[1]
    part = rows // PANEL_SPLIT

    def copies(k):
        slot = k % depth
        return [pltpu.make_async_copy(srcs[k].at[pl.ds(s * part, part), :],
                                      stage_ref.at[slot, pl.ds(s * part, part), :], sem_ref.at[slot])
                for s in range(PANEL_SPLIT)]

    def start(k):
        for cp in copies(k):
            cp.start()

    def prime():
        for k in range(min(depth, len(srcs))):
            start(k)

    def take(k):
        for cp in copies(k):
            cp.wait()
        dsts[k][...] = stage_ref[k % depth].astype(BF16)
        if k + depth < len(srcs):
            start(k + depth)

    return prime, take


def _col_panels(ref):
    return [ref.at[:, pl.ds(p * FF_COLS, FF_COLS)] for p in range(ref.shape[1] // FF_COLS)]


def _row_panels(ref):
    return [ref.at[pl.ds(p * FF_COLS, FF_COLS), :] for p in range(ref.shape[0] // FF_COLS)]


def _load_weights(col_pairs, row_pairs, col_stage, col_sem, row_stage, row_sem):
    col_src = [p for src, _ in col_pairs for p in _col_panels(src)]
    col_dst = [p for _, dst in col_pairs for p in _col_panels(dst)]
    row_src = [p for src, _ in row_pairs for p in _row_panels(src)]
    row_dst = [p for _, dst in row_pairs for p in _row_panels(dst)]
    col_prime, col_take = _panel_stream(col_src, col_dst, col_stage, col_sem)
    row_prime, row_take = _panel_stream(row_src, row_dst, row_stage, row_sem)
    col_prime()
    row_prime()
    n_col, n_row = len(col_src), len(row_src)
    for k in range(max(n_col, n_row)):
        for kc in range(k * n_col // max(n_col, n_row), (k + 1) * n_col // max(n_col, n_row)):
            col_take(kc)
        for kr in range(k * n_row // max(n_col, n_row), (k + 1) * n_row // max(n_col, n_row)):
            row_take(kr)


def _sub_tile_scratch():
    return [pltpu.VMEM((TOKEN_TILE, D_MODEL), BF16), pltpu.VMEM((TOKEN_TILE, D_FF), BF16)]


def _ffn_weight_scratch():
    return [pltpu.VMEM((D_MODEL, D_FF), BF16), pltpu.VMEM((D_MODEL, D_FF), BF16), pltpu.VMEM((D_FF, D_MODEL), BF16),
            pltpu.VMEM((COL_STAGE_DEPTH, D_MODEL, FF_COLS), F32), pltpu.SemaphoreType.DMA((COL_STAGE_DEPTH,)),
            pltpu.VMEM((ROW_STAGE_DEPTH, FF_COLS, D_MODEL), F32), pltpu.SemaphoreType.DMA((ROW_STAGE_DEPTH,))]


def _stage_a_kernel(x_ref, mods_ref, n1_ref, w1_hbm, w3_hbm, w2_hbm, nmix_ref,
                    h1_ref, a2_ref,
                    a_ref, act_ref, w1_ref, w3_ref, w2_ref, col_stage, col_sem, row_stage, row_sem):
    @pl.when(pl.program_id(0) == 0)
    def _():
        _load_weights([(w1_hbm, w1_ref), (w3_hbm, w3_ref)], [(w2_hbm, w2_ref)],
                      col_stage, col_sem, row_stage, row_sem)

    @pl.when(pl.program_id(0) > 0)
    def _():
        mods = mods_ref[0]
        sh1, sc1, g1, sh2, sc2 = mods[0:1], mods[1:2], mods[2:3], mods[3:4], mods[4:5]
        def prologue_rows(_, r0, off, m):
            x = x_ref[r0 + off:r0 + off + m, :]
            return (_rmsnorm(x, n1_ref[...]) * (1.0 + sc1) + sh1).astype(BF16)

        def epilogue(r0, n, ffn):
            h1 = x_ref[r0:r0 + n, :] + (0.5 * g1) * ffn
            h1_ref[r0:r0 + n, :] = h1
            a2_ref[r0:r0 + n, :] = (_rmsnorm(h1, nmix_ref[...]) * (1.0 + sc2) + sh2).astype(BF16)

        _swiglu_sub_tiles(lambda r0, n: None, prologue_rows, epilogue, a_ref, w1_ref, w3_ref, w2_ref, act_ref)


def _token_tile_specs(tile_rows, tiles_per_mod, mod_row0):
    tile_of = lambda i: jnp.maximum(i - 1, 0)
    tile = lambda width: pl.BlockSpec((tile_rows, width), lambda i: (tile_of(i), 0))
    mods = pl.BlockSpec((1, N_MOD, D_MODEL), lambda i: (mod_row0 + tile_of(i) // tiles_per_mod, 0, 0))
    return tile, mods


def _stage_a(x, mods, mod_row0, rows_per_mod, n1, w1, w3, w2, nmix, name):
    tokens, d = x.shape
    tm = TOKEN_TILE
    tile, mods_spec = _token_tile_specs(tm, rows_per_mod // tm, mod_row0)
    return pl.pallas_call(
        _stage_a_kernel,
        out_shape=[jax.ShapeDtypeStruct((tokens, d), F32), jax.ShapeDtypeStruct((tokens, d), BF16)],
        grid=(1 + tokens // tm,),
        in_specs=[tile(d), mods_spec, _resident((1, d)), _hbm(), _hbm(), _hbm(), _resident((1, d))],
        out_specs=[tile(d), tile(d)],
        scratch_shapes=_sub_tile_scratch() + _ffn_weight_scratch(),
        compiler_params=pltpu.CompilerParams(
            dimension_semantics=("arbitrary",), vmem_limit_bytes=V7X_VMEM_LIMIT),
        name=name,
    )(x, mods, n1, w1, w3, w2, nmix)


def _stage_c_kernel(h1_ref, ret_ref, pool_ref, mods_ref, wout_hbm, n2_ref, w1_hbm, w3_hbm, w2_hbm, nf_ref,
                    y_ref,
                    a_ref, act_ref, w1_ref, w3_ref, w2_ref, col_stage, col_sem, row_stage, row_sem, wout_ref):
    @pl.when(pl.program_id(0) == 0)
    def _():
        _load_weights([(wout_hbm, wout_ref), (w1_hbm, w1_ref), (w3_hbm, w3_ref)], [(w2_hbm, w2_ref)],
                      col_stage, col_sem, row_stage, row_sem)

    @pl.when(pl.program_id(0) > 0)
    def _():
        mods = mods_ref[0]
        g2, sh3, sc3, g3 = mods[5:6], mods[6:7], mods[7:8], mods[8:9]
        def prologue_begin(r0, n):
            return (_dot(ret_ref[r0:r0 + n, :], wout_ref[0:RET_WIDTH, :])
                    + _dot(pool_ref[r0:r0 + n, :], wout_ref[RET_WIDTH:, :]))

        def prologue_rows(mix, r0, off, m):
            tile_rows = slice(r0 + off, r0 + off + m)
            h2 = h1_ref[tile_rows, :] + g2 * mix[off:off + m, :]
            y_ref[tile_rows, :] = h2
            return (_rmsnorm(h2, n2_ref[...]) * (1.0 + sc3) + sh3).astype(BF16)

        def epilogue(r0, n, ffn):
            h3 = y_ref[r0:r0 + n, :] + (0.5 * g3) * ffn
            y_ref[r0:r0 + n, :] = _rmsnorm(h3, nf_ref[...])

        _swiglu_sub_tiles(prologue_begin, prologue_rows, epilogue, a_ref, w1_ref, w3_ref, w2_ref, act_ref)


def _stage_c(h1, ret, pool, mods, mod_row0, rows_per_mod, w_out, n2, w1, w3, w2, nf, name):
    tokens, d = h1.shape
    tm = TOKEN_TILE
    tile, mods_spec = _token_tile_specs(tm, rows_per_mod // tm, mod_row0)
    return pl.pallas_call(
        _stage_c_kernel,
        out_shape=jax.ShapeDtypeStruct((tokens, d), F32),
        grid=(1 + tokens // tm,),
        in_specs=[tile(d), tile(RET_WIDTH), tile(RET_WIDTH), mods_spec,
                  _hbm(), _resident((1, d)), _hbm(), _hbm(), _hbm(), _resident((1, d))],
        out_specs=tile(d),
        scratch_shapes=_sub_tile_scratch() + _ffn_weight_scratch() + [pltpu.VMEM(w_out.shape, BF16)],
        compiler_params=pltpu.CompilerParams(
            dimension_semantics=("arbitrary",), vmem_limit_bytes=V7X_VMEM_LIMIT),
        name=name,
    )(h1, ret, pool, mods, w_out, n2, w1, w3, w2, nf)


def _ret_tables(dec_f_ref, dec_b_ref, dmat_ref, qd_ref, kd_ref, cd_ref):
    c = RET_CHUNK
    row = lax.broadcasted_iota(jnp.int32, (c, c), 0)
    col = lax.broadcasted_iota(jnp.int32, (c, c), 1)
    rel = (row - col).astype(F32)
    r = row[:, :HEAD_DIM].astype(F32)
    for h in range(N_HEADS):
        lg_f = -jnp.exp(dec_f_ref[h])
        lg_b = -jnp.exp(dec_b_ref[h])
        dmat_ref[h] = (jnp.where(rel >= 0, jnp.exp(lg_f * jnp.maximum(rel, 0.0)), 0.0)
                       + jnp.where(rel <= 0, jnp.exp(lg_b * jnp.maximum(-rel, 0.0)), 0.0))
        lf, lb = lg_f[:, :HEAD_DIM], lg_b[:, :HEAD_DIM]
        qd_ref[h] = jnp.concatenate([jnp.exp(lf * (r + 1.0)), jnp.exp(lb * (c - r))], axis=1)
        kd_ref[h] = jnp.concatenate([jnp.exp(lf * (c - 1.0 - r)), jnp.exp(lb * r)], axis=1)
        cd_ref[h] = jnp.concatenate([jnp.exp(lf * c), jnp.exp(lb * c)], axis=1)


def _ret_table_scratch():
    c = RET_CHUNK
    return [pltpu.VMEM((N_HEADS, c, c), F32), pltpu.VMEM((N_HEADS, c, 2 * HEAD_DIM), F32),
            pltpu.VMEM((N_HEADS, c, 2 * HEAD_DIM), F32), pltpu.VMEM((N_HEADS, 1, 2 * HEAD_DIM), F32)]


def _project(a2, win_ref, part, rot):
    t = _dot(a2, win_ref[:, part * RET_WIDTH:(part + 1) * RET_WIDTH])
    if part == 1:
        t = t * (HEAD_DIM ** -0.5)
    if part < 2 and rot is not None:
        cos2, sin2 = rot
        t = jnp.concatenate([t[:, _head(h)] * cos2 + pltpu.roll(t[:, _head(h)], HEAD_DIM // 2, axis=1) * sin2
                             for h in range(N_HEADS)], axis=-1)
    return t


def _chunk_kv(k_c, v_c, kd):
    v32 = v_c.astype(F32)
    vd = (jnp.concatenate([v32, v32], axis=1) * kd).astype(BF16)
    return _dot_tn(k_c, vd)


def _ret_out(q_c, k_c, v_c, dmat, cross, gate_c, gn):
    p = (_dot_nt(q_c, k_c) * dmat).astype(BF16)
    o = _dot(p, v_c)
    if cross is not None:
        o = o + cross[:, :HEAD_DIM] + cross[:, HEAD_DIM:]
    o = o * lax.rsqrt(jnp.mean(o * o, axis=-1, keepdims=True) + EPS) * gn
    return (o * _silu(gate_c)).astype(BF16)


def _pool_inv_count(t0, rows, seq_len, window):
    def edge(start):
        t = start + lax.broadcasted_iota(jnp.int32, (POOL_HALO, HEAD_DIM), 0)
        cnt = jnp.minimum(t + window // 2, seq_len) - jnp.maximum(t - window // 2, 0)
        return 1.0 / cnt.astype(F32)
    inner = jnp.full((rows - 2 * POOL_HALO, HEAD_DIM), 1.0 / window, F32)
    return jnp.concatenate([edge(t0), inner, edge(t0 + rows - POOL_HALO)], axis=0)


def _pool_centred(ext, t0, rows, seq_len, window):
    n_ext = rows + 2 * POOL_HALO
    half = window // 2
    acc, span = ext, 1
    while span < half:
        acc = acc + pltpu.roll(acc, n_ext - span, axis=0)
        span *= 2
    acc = acc + pltpu.roll(acc, half, axis=0)
    win = acc[POOL_HALO:POOL_HALO + rows]
    tok = ext[POOL_HALO:POOL_HALO + rows]
    return win * _pool_inv_count(t0, rows, seq_len, window) - tok


def _pool_groups(ext_of_group, pw_ref, ps_ref, t0, rows, seq_len, store):
    for g, window in enumerate(POOL_WINDOWS):
        centred = _pool_centred(ext_of_group(g), t0, rows, seq_len, window)
        store(g, (_dot(centred.astype(BF16), pw_ref[g].astype(BF16)) * ps_ref[:, _head(g)]).astype(BF16))


def _stage_b_context_kernel(a2_ref, win_ref, dec_f_ref, dec_b_ref, gn_ref, pw_ref, ps_ref,
                            ret_ref, pool_ref, sf_ref, sb_ref, dmat_ref, qd_ref, kd_ref, cd_ref, ext_ref,
                            *, seq_len):
    @pl.when(pl.program_id(0) == 0)
    def _():
        _ret_tables(dec_f_ref, dec_b_ref, dmat_ref, qd_ref, kd_ref, cd_ref)
        zeros = jnp.zeros((POOL_HALO, RET_WIDTH), F32)
        for s in range(ext_ref.shape[0]):
            ext_ref[s, 0:POOL_HALO, :] = zeros
            ext_ref[s, POOL_HALO + seq_len:, :] = zeros

    a2 = a2_ref[...]
    q, k, v = (_project(a2, win_ref, part, None).astype(BF16) for part in range(3))
    gate, u = _project(a2, win_ref, 3, None), _project(a2, win_ref, 4, None)
    for s in range(a2_ref.shape[0] // seq_len):
        rows = slice(s * seq_len, (s + 1) * seq_len)
        for h in range(N_HEADS):
            q_c, k_c, v_c = q[rows, _head(h)], k[rows, _head(h)], v[rows, _head(h)]
            ret_ref[rows, _head(h)] = _ret_out(q_c, k_c, v_c, dmat_ref[h], None, gate[rows, _head(h)],
                                               gn_ref[:, _head(h)])
            kv = _chunk_kv(k_c, v_c, kd_ref[h])
            sf_ref[s, 0, h] = kv[:, :HEAD_DIM]
            sb_ref[s, 0, h] = kv[:, HEAD_DIM:]
        ext_ref[s, POOL_HALO:POOL_HALO + seq_len, :] = u[rows, :]

        def store(g, val, rows=rows):
            pool_ref[rows, _head(g)] = val
        _pool_groups(lambda g, s=s: ext_ref[s, :, _head(g)], pw_ref, ps_ref, 0, seq_len, seq_len, store)


def _stage_b_context(a2, w_in, dec_f, dec_b, gn, pool_w, pool_scale, n_seq, seq_len):
    assert seq_len == RET_CHUNK
    tokens = n_seq * seq_len
    tl = RET_TILE
    seqs = tl // seq_len
    tile = lambda width: pl.BlockSpec((tl, width), lambda i: (i, 0))
    state_shape = jax.ShapeDtypeStruct((n_seq, 1, N_HEADS, HEAD_DIM, HEAD_DIM), F32)
    state_spec = lambda: pl.BlockSpec((seqs, 1, N_HEADS, HEAD_DIM, HEAD_DIM), lambda i: (i, 0, 0, 0, 0))
    return pl.pallas_call(
        functools.partial(_stage_b_context_kernel, seq_len=seq_len),
        out_shape=[jax.ShapeDtypeStruct((tokens, RET_WIDTH), BF16)] * 2 + [state_shape] * 2,
        grid=(tokens // tl,),
        in_specs=[tile(D_MODEL), _resident(w_in.shape),
                  _resident(dec_f.shape), _resident(dec_b.shape), _resident(gn.shape),
                  _resident(pool_w.shape), _resident(pool_scale.shape)],
        out_specs=[tile(RET_WIDTH), tile(RET_WIDTH), state_spec(), state_spec()],
        scratch_shapes=_ret_table_scratch() + [pltpu.VMEM((seqs, seq_len + 2 * POOL_HALO, RET_WIDTH), F32)],
        compiler_params=pltpu.CompilerParams(
            dimension_semantics=("arbitrary",), vmem_limit_bytes=V7X_VMEM_LIMIT),
        name="stage_b_context",
    )(a2, w_in, dec_f, dec_b, gn, pool_w, pool_scale)


def _stage_b_latent_kernel(a2_ref, cos_ref, sin_ref, win_ref, dec_f_ref, dec_b_ref, gn_ref, pw_ref, ps_ref,
                           s0f_ref, s0b_ref,
                           ret_ref, pool_ref,
                           dmat_ref, qd_ref, kd_ref, cd_ref, sf_ref, sb_ref, sb_hist_ref, kvf_hist_ref,
                           kseq_ref, vseq_ref, useq_ref,
                           *, seq_len, n_tiles):
    c = RET_CHUNK
    tl = a2_ref.shape[1]
    chunks = tl // c
    j = pl.program_id(1)
    rot = (cos_ref[...], sin_ref[...])

    @pl.when((pl.program_id(0) == 0) & (j == 0))
    def _():
        _ret_tables(dec_f_ref, dec_b_ref, dmat_ref, qd_ref, kd_ref, cd_ref)
        zeros = jnp.zeros((POOL_HALO, RET_WIDTH), F32)
        useq_ref[0:POOL_HALO, :] = zeros
        useq_ref[POOL_HALO + seq_len:, :] = zeros

    @pl.when(j == 0)
    def _():
        for h in range(N_HEADS):
            sb_ref[h] = s0b_ref[0, h]
            sf_ref[h] = s0f_ref[0, h]

    @pl.when(j < n_tiles)
    def _():
        tile = n_tiles - 1 - j
        t0 = pl.multiple_of(tile * tl, tl)
        a2 = a2_ref[0]
        k = _project(a2, win_ref, 1, rot).astype(BF16)
        v = _project(a2, win_ref, 2, rot).astype(BF16)
        kseq_ref[pl.ds(t0, tl), :] = k
        vseq_ref[pl.ds(t0, tl), :] = v
        useq_ref[pl.ds(t0 + POOL_HALO, tl), :] = _project(a2, win_ref, 4, rot)
        for ci in reversed(range(chunks)):
            n = tile * chunks + ci
            rows = slice(ci * c, (ci + 1) * c)
            for h in range(N_HEADS):
                kv = _chunk_kv(k[rows, _head(h)], v[rows, _head(h)], kd_ref[h])
                s_b = sb_ref[h]
                sb_hist_ref[n, h] = s_b.astype(BF16)
                kvf_hist_ref[n, h] = kv[:, :HEAD_DIM]
                sb_ref[h] = s_b * cd_ref[h][:, HEAD_DIM:] + kv[:, HEAD_DIM:]

    @pl.when(j >= n_tiles)
    def _():
        tile = j - n_tiles
        t0 = pl.multiple_of(tile * tl, tl)
        a2 = a2_ref[0]
        q = _project(a2, win_ref, 0, rot).astype(BF16)
        gate = _project(a2, win_ref, 3, rot)
        for ci in range(chunks):
            n = tile * chunks + ci
            rows = slice(ci * c, (ci + 1) * c)
            seq_rows = pl.ds(t0 + ci * c, c)
            for h in range(N_HEADS):
                q_c = q[rows, _head(h)]
                s_f = sf_ref[h]
                s2 = jnp.concatenate([s_f.astype(BF16), sb_hist_ref[n, h]], axis=1)
                cross = _dot(q_c, s2) * qd_ref[h]
                ret_ref[0, rows, _head(h)] = _ret_out(q_c, kseq_ref[seq_rows, _head(h)], vseq_ref[seq_rows, _head(h)],
                                                      dmat_ref[h], cross, gate[rows, _head(h)], gn_ref[:, _head(h)])
                sf_ref[h] = s_f * cd_ref[h][:, :HEAD_DIM] + kvf_hist_ref[n, h]

        def store(g, val):
            pool_ref[0, :, _head(g)] = val
        _pool_groups(lambda g: useq_ref[pl.ds(t0, tl + 2 * POOL_HALO), _head(g)], pw_ref, ps_ref,
                     tile * tl, tl, seq_len, store)


def _stage_b_latent(a2, rot, w_in, dec_f, dec_b, gn, pool_w, pool_scale, s0f, s0b):
    b, l, d = a2.shape
    tl = RET_TILE
    nt = l // tl
    n_chunks = l // RET_CHUNK
    both_tile = lambda j: jnp.where(j < nt, nt - 1 - j, j - nt)
    fwd = lambda: pl.BlockSpec((1, tl, RET_WIDTH), lambda i, j: (i, jnp.maximum(j - nt, 0), 0))
    rot_spec = lambda: pl.BlockSpec((tl, HEAD_DIM), lambda i, j: (both_tile(j), 0))
    state = lambda: pl.BlockSpec((1, N_HEADS, HEAD_DIM, HEAD_DIM), lambda i, j: (i, 0, 0, 0))
    return pl.pallas_call(
        functools.partial(_stage_b_latent_kernel, seq_len=l, n_tiles=nt),
        out_shape=[jax.ShapeDtypeStruct((b, l, RET_WIDTH), BF16)] * 2,
        grid=(b, 2 * nt),
        in_specs=[pl.BlockSpec((1, tl, d), lambda i, j: (i, both_tile(j), 0)), rot_spec(), rot_spec(),
                  _resident(w_in.shape), _resident(dec_f.shape), _resident(dec_b.shape), _resident(gn.shape),
                  _resident(pool_w.shape), _resident(pool_scale.shape), state(), state()],
        out_specs=[fwd(), fwd()],
        scratch_shapes=_ret_table_scratch() + [
            pltpu.VMEM((N_HEADS, HEAD_DIM, HEAD_DIM), F32), pltpu.VMEM((N_HEADS, HEAD_DIM, HEAD_DIM), F32),
            pltpu.VMEM((n_chunks, N_HEADS, HEAD_DIM, HEAD_DIM), BF16),
            pltpu.VMEM((n_chunks, N_HEADS, HEAD_DIM, HEAD_DIM), F32),
            pltpu.VMEM((l, RET_WIDTH), BF16), pltpu.VMEM((l, RET_WIDTH), BF16),
            pltpu.VMEM((l + 2 * POOL_HALO, RET_WIDTH), F32)],
        compiler_params=pltpu.CompilerParams(
            dimension_semantics=("arbitrary", "arbitrary"), vmem_limit_bytes=V7X_VMEM_LIMIT),
        name="stage_b_latent",
    )(a2, *rot, w_in, dec_f, dec_b, gn, pool_w, pool_scale, s0f, s0b)


def _rotary_tables(seq_len):
    rows = seq_len // GRID_W
    row = np.repeat(np.arange(rows, dtype=np.float64), GRID_W)
    col = np.tile(np.arange(GRID_W, dtype=np.float64), rows)
    n_half = HEAD_DIM // 4
    freqs = ROPE_BASE ** (-np.arange(n_half, dtype=np.float64) / n_half)
    ang = np.concatenate([row[:, None] * freqs, col[:, None] * freqs], axis=-1)
    cos, sin = np.cos(ang), np.sin(ang)
    cos2 = np.concatenate([cos, cos], axis=-1).astype(np.float32)
    sin2 = np.concatenate([-sin, sin], axis=-1).astype(np.float32)
    return jnp.asarray(cos2), jnp.asarray(sin2)


def kernel(x_prompt, x_sample, state_ret_fwd, state_ret_bwd, c, c_ctx, ada_w, ada_b, norm_ffn1, ffn1_w1, ffn1_w3, ffn1_w2, norm_mix, w_in, ret_decay_fwd, ret_decay_bwd, ret_gn, pool_w, pool_scale, w_out, norm_ffn2, ffn2_w1, ffn2_w3, ffn2_w2, norm_final):
    depth = ada_w.shape[0]
    assert depth == 1, "single trunk layer"
    d = D_MODEL
    n_ctx, l_ctx, _ = x_prompt.shape
    n_lat, l_lat, _ = x_sample.shape
    t_ctx, t_lat = n_ctx * l_ctx, n_lat * l_lat

    cond = jnp.concatenate([c_ctx[None, :], c, jnp.zeros((8 - 1 - n_lat, d), F32)], axis=0)
    mods = _mods(cond, ada_w[0], ada_b[0])

    row = lambda g: g.reshape(1, -1)
    ffn1 = (ffn1_w1[0], ffn1_w3[0], ffn1_w2[0])
    ffn2 = (ffn2_w1[0], ffn2_w3[0], ffn2_w2[0])
    w_in_b = w_in[0].astype(BF16)
    n1, nmix, n2, nf = row(norm_ffn1[0]), row(norm_mix[0]), row(norm_ffn2[0]), row(norm_final)
    dec_f = jnp.broadcast_to(ret_decay_fwd[0][:, None, None], (N_HEADS, 1, RET_CHUNK))
    dec_b = jnp.broadcast_to(ret_decay_bwd[0][:, None, None], (N_HEADS, 1, RET_CHUNK))
    gn, ps = row(ret_gn[0]), row(pool_scale[0])

    h1c, a2c = _stage_a(x_prompt.reshape(t_ctx, d), mods, 0, t_ctx, n1, *ffn1, nmix, "stage_a_context")
    retc, poolc, new_f, new_b = _stage_b_context(a2c, w_in_b, dec_f, dec_b, gn, pool_w[0], ps, n_ctx, l_ctx)
    y_prompt = _stage_c(h1c, retc, poolc, mods, 0, t_ctx, w_out[0], n2, *ffn2, nf, "stage_c_context")

    h1l, a2l = _stage_a(x_sample.reshape(t_lat, d), mods, 1, l_lat, n1, *ffn1, nmix, "stage_a_latent")
    retl, pooll = _stage_b_latent(a2l.reshape(n_lat, l_lat, d), _rotary_tables(l_lat), w_in_b, dec_f, dec_b, gn,
                                  pool_w[0], ps, state_ret_fwd[:, 0], state_ret_bwd[:, 0])
    y_sample = _stage_c(h1l, retl.reshape(t_lat, RET_WIDTH), pooll.reshape(t_lat, RET_WIDTH), mods, 1, l_lat,
                        w_out[0], n2, *ffn2, nf, "stage_c_latent")

    return (y_prompt.reshape(n_ctx, l_ctx, d), y_sample.reshape(n_lat, l_lat, d), new_f, new_b)
```

```python
import functools

import jax
import jax.numpy as jnp
import numpy as np
from jax import lax
from jax.experimental import pallas as pl
from jax.experimental.pallas import tpu as pltpu

D_MODEL = 1024
GRID_W = 64
N_HEADS = 4
HEAD_DIM = 128
RET_WIDTH = N_HEADS * HEAD_DIM
POOL_WINDOWS = (2, 4, 8, 16)
POOL_HALO = 8
D_FF = 2816
ROPE_BASE = 10000.0
N_MOD = 9
EPS = 1e-6

FF_COLS = 256
SUB_TILE = 512
FFN_SUB_TILES = 2
TOKEN_TILE = SUB_TILE * FFN_SUB_TILES
RET_TILE = 1024
RET_CHUNK = 256
COL_STAGE_DEPTH = 4
ROW_STAGE_DEPTH = 2
PANEL_SPLIT = 4
V7X_VMEM_LIMIT = 56 * 1024 * 1024

F32 = jnp.float32
BF16 = jnp.bfloat16


def _silu(x):
    return x * (1.0 / (1.0 + jnp.exp(-x)))


def _rmsnorm(x, g):
    return x * lax.rsqrt(jnp.mean(x * x, axis=-1, keepdims=True) + EPS) * g


def _dot(a, b):
    return jnp.dot(a, b, preferred_element_type=F32)


def _dot_tn(a, b):
    return lax.dot_general(a, b, (((0,), (0,)), ((), ())), preferred_element_type=F32)


def _dot_nt(a, b):
    return lax.dot_general(a, b, (((1,), (1,)), ((), ())), preferred_element_type=F32)


def _head(h):
    return slice(h * HEAD_DIM, (h + 1) * HEAD_DIM)


def _resident(shape):
    nd = len(shape)
    return pl.BlockSpec(shape, lambda *_: (0,) * nd, pipeline_mode=pl.Buffered(1))


def _hbm():
    return pl.BlockSpec(memory_space=pl.ANY)


def _mods_kernel(cond_ref, w_ref, b_ref, o_ref):
    s = _silu(cond_ref[...]).astype(BF16)
    o_ref[...] = _dot(s, w_ref[...].astype(BF16)) + b_ref[...]


def _mods(cond, ada_w, ada_b):
    n = N_MOD * D_MODEL
    out = pl.pallas_call(
        _mods_kernel,
        out_shape=jax.ShapeDtypeStruct((8, n), F32),
        grid=(N_MOD,),
        in_specs=[
            pl.BlockSpec((8, D_MODEL), lambda j: (0, 0)),
            pl.BlockSpec((D_MODEL, D_MODEL), lambda j: (0, j)),
            pl.BlockSpec((1, D_MODEL), lambda j: (0, j)),
        ],
        out_specs=pl.BlockSpec((8, D_MODEL), lambda j: (0, j)),
        name="adaln_mods",
    )(cond, ada_w, ada_b.reshape(1, n))
    return out.reshape(8, N_MOD, D_MODEL)


def _swiglu(a_bf16, w1_ref, w3_ref, w2_ref, act_ref):
    for c in range(D_FF // FF_COLS):
        cols = slice(c * FF_COLS, (c + 1) * FF_COLS)
        g = _dot(a_bf16, w1_ref[:, cols])
        u = _dot(a_bf16, w3_ref[:, cols])
        act_ref[:, cols] = (_silu(g) * u).astype(BF16)
    return _dot(act_ref[...], w2_ref[...])


def _panel_stream(srcs, dsts, stage_ref, sem_ref):
    depth, rows = stage_ref.shape[0], stage_ref.shape[1]
    part = rows // PANEL_SPLIT

    def copies(k):
        slot = k % depth
        return [pltpu.make_async_copy(srcs[k].at[pl.ds(s * part, part), :],
                                      stage_ref.at[slot, pl.ds(s * part, part), :], sem_ref.at[slot])
                for s in range(PANEL_SPLIT)]

    def start(k):
        for cp in copies(k):
            cp.start()

    def prime():
        for k in range(min(depth, len(srcs))):
            start(k)

    def take(k):
        for cp in copies(k):
            cp.wait()
        dsts[k][...] = stage_ref[k % depth].astype(BF16)
        if k + depth < len(srcs):
            start(k + depth)

    return prime, take


def _col_panels(ref):
    return [ref.at[:, pl.ds(p * FF_COLS, FF_COLS)] for p in range(ref.shape[1] // FF_COLS)]


def _row_panels(ref):
    return [ref.at[pl.ds(p * FF_COLS, FF_COLS), :] for p in range(ref.shape[0] // FF_COLS)]


def _load_weights(col_pairs, row_pairs, col_stage, col_sem, row_stage, row_sem):
    col_src = [p for src, _ in col_pairs for p in _col_panels(src)]
    col_dst = [p for _, dst in col_pairs for p in _col_panels(dst)]
    row_src = [p for src, _ in row_pairs for p in _row_panels(src)]
    row_dst = [p for _, dst in row_pairs for p in _row_panels(dst)]
    col_prime, col_take = _panel_stream(col_src, col_dst, col_stage, col_sem)
    row_prime, row_take = _panel_stream(row_src, row_dst, row_stage, row_sem)
    col_prime()
    row_prime()
    n_col, n_row = len(col_src), len(row_src)
    for k in range(max(n_col, n_row)):
        for kc in range(k * n_col // max(n_col, n_row), (k + 1) * n_col // max(n_col, n_row)):
            col_take(kc)
        for kr in range(k * n_row // max(n_col, n_row), (k + 1) * n_row // max(n_col, n_row)):
            row_take(kr)


def _export_copies(vmem_refs, hbm_refs, sem_ref):
    return [pltpu.make_async_copy(src, dst, sem_ref.at[k]) for k, (src, dst) in enumerate(zip(vmem_refs, hbm_refs))]


def _converting_call(compute, hbm_weights, col_major, vmem_weights, exports, stage_refs, export_sem):
    step = pl.program_id(0)
    pairs = list(zip(hbm_weights, vmem_weights))

    @pl.when(step == 0)
    def _():
        _load_weights([p for p, c in zip(pairs, col_major) if c], [p for p, c in zip(pairs, col_major) if not c],
                      *stage_refs)
        for cp in _export_copies(vmem_weights, exports, export_sem):
            cp.start()

    @pl.when(step > 0)
    def _():
        compute()

    @pl.when(step == pl.num_programs(0) - 1)
    def _():
        for cp in _export_copies(vmem_weights, exports, export_sem):
            cp.wait()


def _stage_a_compute(x_ref, mods_ref, n1_ref, nmix_ref, h1_ref, a2_ref, act_ref, w1_ref, w3_ref, w2_ref):
    mods = mods_ref[0]
    sh1, sc1, g1, sh2, sc2 = mods[0:1], mods[1:2], mods[2:3], mods[3:4], mods[4:5]
    for s in range(FFN_SUB_TILES):
        rows = slice(s * SUB_TILE, (s + 1) * SUB_TILE)
        x = x_ref[rows, :]
        a1 = _rmsnorm(x, n1_ref[...]) * (1.0 + sc1) + sh1
        h1 = x + (0.5 * g1) * _swiglu(a1.astype(BF16), w1_ref, w3_ref, w2_ref, act_ref.at[s])
        h1_ref[rows, :] = h1
        a2_ref[rows, :] = (_rmsnorm(h1, nmix_ref[...]) * (1.0 + sc2) + sh2).astype(BF16)


def _stage_a_converting_kernel(x_ref, mods_ref, n1_ref, w1_hbm, w3_hbm, w2_hbm, nmix_ref,
                               h1_ref, a2_ref, w1_out, w3_out, w2_out,
                               act_ref, w1_ref, w3_ref, w2_ref, col_stage, col_sem, row_stage, row_sem, export_sem):
    compute = functools.partial(_stage_a_compute, x_ref, mods_ref, n1_ref, nmix_ref, h1_ref, a2_ref, act_ref,
                                w1_ref, w3_ref, w2_ref)
    _converting_call(compute, (w1_hbm, w3_hbm, w2_hbm), (True, True, False), (w1_ref, w3_ref, w2_ref),
                     (w1_out, w3_out, w2_out), (col_stage, col_sem, row_stage, row_sem), export_sem)


def _stage_a_kernel(x_ref, mods_ref, n1_ref, w1_ref, w3_ref, w2_ref, nmix_ref, h1_ref, a2_ref, act_ref):
    _stage_a_compute(x_ref, mods_ref, n1_ref, nmix_ref, h1_ref, a2_ref, act_ref, w1_ref, w3_ref, w2_ref)


def _stage_c_compute(h1_ref, ret_ref, pool_ref, mods_ref, n2_ref, nf_ref, y_ref, act_ref,
                     wout_ref, w1_ref, w3_ref, w2_ref):
    mods = mods_ref[0]
    g2, sh3, sc3, g3 = mods[5:6], mods[6:7], mods[7:8], mods[8:9]
    for s in range(FFN_SUB_TILES):
        rows = slice(s * SUB_TILE, (s + 1) * SUB_TILE)
        mix = (_dot(ret_ref[rows, :], wout_ref[0:RET_WIDTH, :])
               + _dot(pool_ref[rows, :], wout_ref[RET_WIDTH:, :]))
        h2 = h1_ref[rows, :] + g2 * mix
        a3 = _rmsnorm(h2, n2_ref[...]) * (1.0 + sc3) + sh3
        h3 = h2 + (0.5 * g3) * _swiglu(a3.astype(BF16), w1_ref, w3_ref, w2_ref, act_ref.at[s])
        y_ref[rows, :] = _rmsnorm(h3, nf_ref[...])


def _stage_c_converting_kernel(h1_ref, ret_ref, pool_ref, mods_ref, wout_hbm, n2_ref, w1_hbm, w3_hbm, w2_hbm, nf_ref,
                               y_ref, wout_out, w1_out, w3_out, w2_out,
                               act_ref, w1_ref, w3_ref, w2_ref, col_stage, col_sem, row_stage, row_sem, export_sem,
                               wout_ref):
    compute = functools.partial(_stage_c_compute, h1_ref, ret_ref, pool_ref, mods_ref, n2_ref, nf_ref, y_ref, act_ref,
                                wout_ref, w1_ref, w3_ref, w2_ref)
    _converting_call(compute, (wout_hbm, w1_hbm, w3_hbm, w2_hbm), (True, True, True, False),
                     (wout_ref, w1_ref, w3_ref, w2_ref), (wout_out, w1_out, w3_out, w2_out),
                     (col_stage, col_sem, row_stage, row_sem), export_sem)


def _stage_c_kernel(h1_ref, ret_ref, pool_ref, mods_ref, wout_ref, n2_ref, w1_ref, w3_ref, w2_ref, nf_ref,
                    y_ref, act_ref):
    _stage_c_compute(h1_ref, ret_ref, pool_ref, mods_ref, n2_ref, nf_ref, y_ref, act_ref,
                     wout_ref, w1_ref, w3_ref, w2_ref)


def _ffn_stage_specs(weight, mod_row0, rows_per_mod):
    converting = weight.dtype == F32
    lead = 1 if converting else 0
    tile_of = lambda i: jnp.maximum(i - lead, 0)
    tile = lambda width: pl.BlockSpec((TOKEN_TILE, width), lambda i: (tile_of(i), 0))
    tiles_per_mod = rows_per_mod // TOKEN_TILE
    mods_spec = pl.BlockSpec((1, N_MOD, D_MODEL), lambda i: (mod_row0 + tile_of(i) // tiles_per_mod, 0, 0))
    weight_spec = (lambda w: _hbm()) if converting else (lambda w: _resident(w.shape))
    return tile, mods_spec, weight_spec, converting, lead


def _act_scratch():
    return [pltpu.VMEM((FFN_SUB_TILES, SUB_TILE, D_FF), BF16)]


def _conversion_scratch(weights):
    return ([pltpu.VMEM(w.shape, BF16) for w in weights]
            + [pltpu.VMEM((COL_STAGE_DEPTH, D_MODEL, FF_COLS), F32), pltpu.SemaphoreType.DMA((COL_STAGE_DEPTH,)),
               pltpu.VMEM((ROW_STAGE_DEPTH, FF_COLS, D_MODEL), F32), pltpu.SemaphoreType.DMA((ROW_STAGE_DEPTH,)),
               pltpu.SemaphoreType.DMA((len(weights),))])


def _stage_a(x, mods, mod_row0, rows_per_mod, n1, w1, w3, w2, nmix, name):
    tile, mods_spec, weight_spec, converting, lead = _ffn_stage_specs(w1, mod_row0, rows_per_mod)
    tokens, d, tm = x.shape[0], D_MODEL, TOKEN_TILE
    out_shape = [jax.ShapeDtypeStruct((tokens, d), F32), jax.ShapeDtypeStruct((tokens, d), BF16)]
    out_specs = [tile(d), tile(d)]
    scratch = _act_scratch()
    if converting:
        out_shape += [jax.ShapeDtypeStruct(w.shape, BF16) for w in (w1, w3, w2)]
        out_specs += [_hbm()] * 3
        scratch += _conversion_scratch((w1, w3, w2))
    outs = pl.pallas_call(
        _stage_a_converting_kernel if converting else _stage_a_kernel,
        out_shape=out_shape,
        grid=(lead + tokens // tm,),
        in_specs=[tile(d), mods_spec, _resident((1, d)), weight_spec(w1), weight_spec(w3), weight_spec(w2),
                  _resident((1, d))],
        out_specs=out_specs,
        scratch_shapes=scratch,
        compiler_params=pltpu.CompilerParams(
            dimension_semantics=("arbitrary",), vmem_limit_bytes=V7X_VMEM_LIMIT),
        name=name,
    )(x, mods, n1, w1, w3, w2, nmix)
    return outs[0], outs[1], tuple(outs[2:])


def _stage_c(h1, ret, pool, mods, mod_row0, rows_per_mod, w_out, n2, w1, w3, w2, nf, name):
    tile, mods_spec, weight_spec, converting, lead = _ffn_stage_specs(w1, mod_row0, rows_per_mod)
    tokens, d, tm = h1.shape[0], D_MODEL, TOKEN_TILE
    out_shape = [jax.ShapeDtypeStruct((tokens, d), F32)]
    out_specs = [tile(d)]
    scratch = _act_scratch()
    if converting:
        out_shape += [jax.ShapeDtypeStruct(w.shape, BF16) for w in (w_out, w1, w3, w2)]
        out_specs += [_hbm()] * 4
        conv = _conversion_scratch((w1, w3, w2))
        conv[-1] = pltpu.SemaphoreType.DMA((4,))
        scratch += conv + [pltpu.VMEM(w_out.shape, BF16)]
    outs = pl.pallas_call(
        _stage_c_converting_kernel if converting else _stage_c_kernel,
        out_shape=out_shape,
        grid=(lead + tokens // tm,),
        in_specs=[tile(d), tile(RET_WIDTH), tile(RET_WIDTH), mods_spec,
                  weight_spec(w_out), _resident((1, d)), weight_spec(w1), weight_spec(w3), weight_spec(w2),
                  _resident((1, d))],
        out_specs=out_specs,
        scratch_shapes=scratch,
        compiler_params=pltpu.CompilerParams(
            dimension_semantics=("arbitrary",), vmem_limit_bytes=V7X_VMEM_LIMIT),
        name=name,
    )(h1, ret, pool, mods, w_out, n2, w1, w3, w2, nf)
    return outs[0], tuple(outs[1:])


def _ret_tables(dec_f_ref, dec_b_ref, dmat_ref, qd_ref, kd_ref, cd_ref):
    c = RET_CHUNK
    row = lax.broadcasted_iota(jnp.int32, (c, c), 0)
    col = lax.broadcasted_iota(jnp.int32, (c, c), 1)
    rel = (row - col).astype(F32)
    r = row[:, :HEAD_DIM].astype(F32)
    for h in range(N_HEADS):
        lg_f = -jnp.exp(dec_f_ref[h])
        lg_b = -jnp.exp(dec_b_ref[h])
        dmat_ref[h] = (jnp.where(rel >= 0, jnp.exp(lg_f * jnp.maximum(rel, 0.0)), 0.0)
                       + jnp.where(rel <= 0, jnp.exp(lg_b * jnp.maximum(-rel, 0.0)), 0.0))
        lf, lb = lg_f[:, :HEAD_DIM], lg_b[:, :HEAD_DIM]
        qd_ref[h] = jnp.concatenate([jnp.exp(lf * (r + 1.0)), jnp.exp(lb * (c - r))], axis=1)
        kd_ref[h] = jnp.concatenate([jnp.exp(lf * (c - 1.0 - r)), jnp.exp(lb * r)], axis=1)
        cd_ref[h] = jnp.concatenate([jnp.exp(lf * c), jnp.exp(lb * c)], axis=1)


def _ret_table_scratch():
    c = RET_CHUNK
    return [pltpu.VMEM((N_HEADS, c, c), F32), pltpu.VMEM((N_HEADS, c, 2 * HEAD_DIM), F32),
            pltpu.VMEM((N_HEADS, c, 2 * HEAD_DIM), F32), pltpu.VMEM((N_HEADS, 1, 2 * HEAD_DIM), F32)]


def _project(a2, win_ref, part, rot):
    t = _dot(a2, win_ref[:, part * RET_WIDTH:(part + 1) * RET_WIDTH])
    if part == 1:
        t = t * (HEAD_DIM ** -0.5)
    if part < 2 and rot is not None:
        cos2, sin2 = rot
        t = jnp.concatenate([t[:, _head(h)] * cos2 + pltpu.roll(t[:, _head(h)], HEAD_DIM // 2, axis=1) * sin2
                             for h in range(N_HEADS)], axis=-1)
    return t


def _chunk_kv(k_c, v_c, kd):
    v32 = v_c.astype(F32)
    vd = (jnp.concatenate([v32, v32], axis=1) * kd).astype(BF16)
    return _dot_tn(k_c, vd)


def _ret_out(q_c, k_c, v_c, dmat, cross, gate_c, gn):
    p = (_dot_nt(q_c, k_c) * dmat).astype(BF16)
    o = _dot(p, v_c)
    if cross is not None:
        o = o + cross[:, :HEAD_DIM] + cross[:, HEAD_DIM:]
    o = o * lax.rsqrt(jnp.mean(o * o, axis=-1, keepdims=True) + EPS) * gn
    return (o * _silu(gate_c)).astype(BF16)


def _pool_inv_count(t0, rows, seq_len, window):
    def edge(start):
        t = start + lax.broadcasted_iota(jnp.int32, (POOL_HALO, HEAD_DIM), 0)
        cnt = jnp.minimum(t + window // 2, seq_len) - jnp.maximum(t - window // 2, 0)
        return 1.0 / cnt.astype(F32)
    inner = jnp.full((rows - 2 * POOL_HALO, HEAD_DIM), 1.0 / window, F32)
    return jnp.concatenate([edge(t0), inner, edge(t0 + rows - POOL_HALO)], axis=0)


def _pool_centred(ext, t0, rows, seq_len, window):
    n_ext = rows + 2 * POOL_HALO
    half = window // 2
    acc, span = ext, 1
    while span < half:
        acc = acc + pltpu.roll(acc, n_ext - span, axis=0)
        span *= 2
    acc = acc + pltpu.roll(acc, half, axis=0)
    win = acc[POOL_HALO:POOL_HALO + rows]
    tok = ext[POOL_HALO:POOL_HALO + rows]
    return win * _pool_inv_count(t0, rows, seq_len, window) - tok


def _pool_groups(ext_of_group, pw_ref, ps_ref, t0, rows, seq_len, store):
    for g, window in enumerate(POOL_WINDOWS):
        centred = _pool_centred(ext_of_group(g), t0, rows, seq_len, window)
        store(g, (_dot(centred.astype(BF16), pw_ref[g].astype(BF16)) * ps_ref[:, _head(g)]).astype(BF16))


def _stage_b_context_kernel(a2_ref, win_ref, dec_f_ref, dec_b_ref, gn_ref, pw_ref, ps_ref,
                            ret_ref, pool_ref, sf_ref, sb_ref, dmat_ref, qd_ref, kd_ref, cd_ref, ext_ref,
                            *, seq_len):
    @pl.when(pl.program_id(0) == 0)
    def _():
        _ret_tables(dec_f_ref, dec_b_ref, dmat_ref, qd_ref, kd_ref, cd_ref)
        zeros = jnp.zeros((POOL_HALO, RET_WIDTH), F32)
        for s in range(ext_ref.shape[0]):
            ext_ref[s, 0:POOL_HALO, :] = zeros
            ext_ref[s, POOL_HALO + seq_len:, :] = zeros

    a2 = a2_ref[...]
    q, k, v = (_project(a2, win_ref, part, None).astype(BF16) for part in range(3))
    gate, u = _project(a2, win_ref, 3, None), _project(a2, win_ref, 4, None)
    for s in range(a2_ref.shape[0] // seq_len):
        rows = slice(s * seq_len, (s + 1) * seq_len)
        for h in range(N_HEADS):
            q_c, k_c, v_c = q[rows, _head(h)], k[rows, _head(h)], v[rows, _head(h)]
            ret_ref[rows, _head(h)] = _ret_out(q_c, k_c, v_c, dmat_ref[h], None, gate[rows, _head(h)],
                                               gn_ref[:, _head(h)])
            kv = _chunk_kv(k_c, v_c, kd_ref[h])
            sf_ref[s, 0, h] = kv[:, :HEAD_DIM]
            sb_ref[s, 0, h] = kv[:, HEAD_DIM:]
        ext_ref[s, POOL_HALO:POOL_HALO + seq_len, :] = u[rows, :]

        def store(g, val, rows=rows):
            pool_ref[rows, _head(g)] = val
        _pool_groups(lambda g, s=s: ext_ref[s, :, _head(g)], pw_ref, ps_ref, 0, seq_len, seq_len, store)


def _stage_b_context(a2, w_in, dec_f, dec_b, gn, pool_w, pool_scale, n_seq, seq_len):
    assert seq_len == RET_CHUNK
    tokens = n_seq * seq_len
    tl = RET_TILE
    seqs = tl // seq_len
    tile = lambda width: pl.BlockSpec((tl, width), lambda i: (i, 0))
    state_shape = jax.ShapeDtypeStruct((n_seq, 1, N_HEADS, HEAD_DIM, HEAD_DIM), F32)
    state_spec = lambda: pl.BlockSpec((seqs, 1, N_HEADS, HEAD_DIM, HEAD_DIM), lambda i: (i, 0, 0, 0, 0))
    return pl.pallas_call(
        functools.partial(_stage_b_context_kernel, seq_len=seq_len),
        out_shape=[jax.ShapeDtypeStruct((tokens, RET_WIDTH), BF16)] * 2 + [state_shape] * 2,
        grid=(tokens // tl,),
        in_specs=[tile(D_MODEL), _resident(w_in.shape),
                  _resident(dec_f.shape), _resident(dec_b.shape), _resident(gn.shape),
                  _resident(pool_w.shape), _resident(pool_scale.shape)],
        out_specs=[tile(RET_WIDTH), tile(RET_WIDTH), state_spec(), state_spec()],
        scratch_shapes=_ret_table_scratch() + [pltpu.VMEM((seqs, seq_len + 2 * POOL_HALO, RET_WIDTH), F32)],
        compiler_params=pltpu.CompilerParams(
            dimension_semantics=("arbitrary",), vmem_limit_bytes=V7X_VMEM_LIMIT),
        name="stage_b_context",
    )(a2, w_in, dec_f, dec_b, gn, pool_w, pool_scale)


def _stage_b_latent_kernel(a2_ref, cos_ref, sin_ref, win_ref, dec_f_ref, dec_b_ref, gn_ref, pw_ref, ps_ref,
                           s0f_ref, s0b_ref,
                           ret_ref, pool_ref,
                           dmat_ref, qd_ref, kd_ref, cd_ref, sf_ref, sb_ref, sb_hist_ref, kvf_hist_ref,
                           kseq_ref, vseq_ref, useq_ref,
                           *, seq_len, n_tiles):
    c = RET_CHUNK
    tl = a2_ref.shape[1]
    chunks = tl // c
    j = pl.program_id(1)
    rot = (cos_ref[...], sin_ref[...])

    @pl.when((pl.program_id(0) == 0) & (j == 0))
    def _():
        _ret_tables(dec_f_ref, dec_b_ref, dmat_ref, qd_ref, kd_ref, cd_ref)
        zeros = jnp.zeros((POOL_HALO, RET_WIDTH), F32)
        useq_ref[0:POOL_HALO, :] = zeros
        useq_ref[POOL_HALO + seq_len:, :] = zeros

    @pl.when(j == 0)
    def _():
        for h in range(N_HEADS):
            sb_ref[h] = s0b_ref[0, h]
            sf_ref[h] = s0f_ref[0, h]

    @pl.when(j < n_tiles)
    def _():
        tile = n_tiles - 1 - j
        t0 = pl.multiple_of(tile * tl, tl)
        a2 = a2_ref[0]
        k = _project(a2, win_ref, 1, rot).astype(BF16)
        v = _project(a2, win_ref, 2, rot).astype(BF16)
        kseq_ref[pl.ds(t0, tl), :] = k
        vseq_ref[pl.ds(t0, tl), :] = v
        useq_ref[pl.ds(t0 + POOL_HALO, tl), :] = _project(a2, win_ref, 4, rot)
        for ci in reversed(range(chunks)):
            n = tile * chunks + ci
            rows = slice(ci * c, (ci + 1) * c)
            for h in range(N_HEADS):
                kv = _chunk_kv(k[rows, _head(h)], v[rows, _head(h)], kd_ref[h])
                s_b = sb_ref[h]
                sb_hist_ref[n, h] = s_b.astype(BF16)
                kvf_hist_ref[n, h] = kv[:, :HEAD_DIM]
                sb_ref[h] = s_b * cd_ref[h][:, HEAD_DIM:] + kv[:, HEAD_DIM:]

    @pl.when(j >= n_tiles)
    def _():
        tile = j - n_tiles
        t0 = pl.multiple_of(tile * tl, tl)
        a2 = a2_ref[0]
        q = _project(a2, win_ref, 0, rot).astype(BF16)
        gate = _project(a2, win_ref, 3, rot)
        for ci in range(chunks):
            n = tile * chunks + ci
            rows = slice(ci * c, (ci + 1) * c)
            seq_rows = pl.ds(t0 + ci * c, c)
            for h in range(N_HEADS):
                q_c = q[rows, _head(h)]
                s_f = sf_ref[h]
                s2 = jnp.concatenate([s_f.astype(BF16), sb_hist_ref[n, h]], axis=1)
                cross = _dot(q_c, s2) * qd_ref[h]
                ret_ref[0, rows, _head(h)] = _ret_out(q_c, kseq_ref[seq_rows, _head(h)], vseq_ref[seq_rows, _head(h)],
                                                      dmat_ref[h], cross, gate[rows, _head(h)], gn_ref[:, _head(h)])
                sf_ref[h] = s_f * cd_ref[h][:, :HEAD_DIM] + kvf_hist_ref[n, h]

        def store(g, val):
            pool_ref[0, :, _head(g)] = val
        _pool_groups(lambda g: useq_ref[pl.ds(t0, tl + 2 * POOL_HALO), _head(g)], pw_ref, ps_ref,
                     tile * tl, tl, seq_len, store)


def _stage_b_latent(a2, rot, w_in, dec_f, dec_b, gn, pool_w, pool_scale, s0f, s0b):
    b, l, d = a2.shape
    tl = RET_TILE
    nt = l // tl
    n_chunks = l // RET_CHUNK
    both_tile = lambda j: jnp.where(j < nt, nt - 1 - j, j - nt)
    fwd = lambda: pl.BlockSpec((1, tl, RET_WIDTH), lambda i, j: (i, jnp.maximum(j - nt, 0), 0))
    rot_spec = lambda: pl.BlockSpec((tl, HEAD_DIM), lambda i, j: (both_tile(j), 0))
    state = lambda: pl.BlockSpec((1, N_HEADS, HEAD_DIM, HEAD_DIM), lambda i, j: (i, 0, 0, 0))
    return pl.pallas_call(
        functools.partial(_stage_b_latent_kernel, seq_len=l, n_tiles=nt),
        out_shape=[jax.ShapeDtypeStruct((b, l, RET_WIDTH), BF16)] * 2,
        grid=(b, 2 * nt),
        in_specs=[pl.BlockSpec((1, tl, d), lambda i, j: (i, both_tile(j), 0)), rot_spec(), rot_spec(),
                  _resident(w_in.shape), _resident(dec_f.shape), _resident(dec_b.shape), _resident(gn.shape),
                  _resident(pool_w.shape), _resident(pool_scale.shape), state(), state()],
        out_specs=[fwd(), fwd()],
        scratch_shapes=_ret_table_scratch() + [
            pltpu.VMEM((N_HEADS, HEAD_DIM, HEAD_DIM), F32), pltpu.VMEM((N_HEADS, HEAD_DIM, HEAD_DIM), F32),
            pltpu.VMEM((n_chunks, N_HEADS, HEAD_DIM, HEAD_DIM), BF16),
            pltpu.VMEM((n_chunks, N_HEADS, HEAD_DIM, HEAD_DIM), F32),
            pltpu.VMEM((l, RET_WIDTH), BF16), pltpu.VMEM((l, RET_WIDTH), BF16),
            pltpu.VMEM((l + 2 * POOL_HALO, RET_WIDTH), F32)],
        compiler_params=pltpu.CompilerParams(
            dimension_semantics=("arbitrary", "arbitrary"), vmem_limit_bytes=V7X_VMEM_LIMIT),
        name="stage_b_latent",
    )(a2, *rot, w_in, dec_f, dec_b, gn, pool_w, pool_scale, s0f, s0b)


def _rotary_tables(seq_len):
    rows = seq_len // GRID_W
    row = np.repeat(np.arange(rows, dtype=np.float64), GRID_W)
    col = np.tile(np.arange(GRID_W, dtype=np.float64), rows)
    n_half = HEAD_DIM // 4
    freqs = ROPE_BASE ** (-np.arange(n_half, dtype=np.float64) / n_half)
    ang = np.concatenate([row[:, None] * freqs, col[:, None] * freqs], axis=-1)
    cos, sin = np.cos(ang), np.sin(ang)
    cos2 = np.concatenate([cos, cos], axis=-1).astype(np.float32)
    sin2 = np.concatenate([-sin, sin], axis=-1).astype(np.float32)
    return jnp.asarray(cos2), jnp.asarray(sin2)


def kernel(x_prompt, x_sample, state_ret_fwd, state_ret_bwd, c, c_ctx, ada_w, ada_b, norm_ffn1, ffn1_w1, ffn1_w3, ffn1_w2, norm_mix, w_in, ret_decay_fwd, ret_decay_bwd, ret_gn, pool_w, pool_scale, w_out, norm_ffn2, ffn2_w1, ffn2_w3, ffn2_w2, norm_final):
    depth = ada_w.shape[0]
    assert depth == 1, "single trunk layer"
    d = D_MODEL
    n_ctx, l_ctx, _ = x_prompt.shape
    n_lat, l_lat, _ = x_sample.shape
    t_ctx, t_lat = n_ctx * l_ctx, n_lat * l_lat

    cond = jnp.concatenate([c_ctx[None, :], c, jnp.zeros((8 - 1 - n_lat, d), F32)], axis=0)
    mods = _mods(cond, ada_w[0], ada_b[0])

    row = lambda g: g.reshape(1, -1)
    w_in_b = w_in[0].astype(BF16)
    n1, nmix, n2, nf = row(norm_ffn1[0]), row(norm_mix[0]), row(norm_ffn2[0]), row(norm_final)
    dec_f = jnp.broadcast_to(ret_decay_fwd[0][:, None, None], (N_HEADS, 1, RET_CHUNK))
    dec_b = jnp.broadcast_to(ret_decay_bwd[0][:, None, None], (N_HEADS, 1, RET_CHUNK))
    gn, ps = row(ret_gn[0]), row(pool_scale[0])

    h1c, a2c, ffn1_b = _stage_a(x_prompt.reshape(t_ctx, d), mods, 0, t_ctx, n1, ffn1_w1[0], ffn1_w3[0], ffn1_w2[0],
                                nmix, "stage_a_context")
    retc, poolc, new_f, new_b = _stage_b_context(a2c, w_in_b, dec_f, dec_b, gn, pool_w[0], ps, n_ctx, l_ctx)
    y_prompt, (w_out_b, *ffn2_b) = _stage_c(h1c, retc, poolc, mods, 0, t_ctx, w_out[0], n2,
                                           ffn2_w1[0], ffn2_w3[0], ffn2_w2[0], nf, "stage_c_context")

    h1l, a2l, _ = _stage_a(x_sample.reshape(t_lat, d), mods, 1, l_lat, n1, *ffn1_b, nmix, "stage_a_latent")
    retl, pooll = _stage_b_latent(a2l.reshape(n_lat, l_lat, d), _rotary_tables(l_lat), w_in_b, dec_f, dec_b, gn,
                                  pool_w[0], ps, state_ret_fwd[:, 0], state_ret_bwd[:, 0])
    y_sample, _ = _stage_c(h1l, retl.reshape(t_lat, RET_WIDTH), pooll.reshape(t_lat, RET_WIDTH), mods, 1, l_lat,
                           w_out_b, n2, *ffn2_b, nf, "stage_c_latent")

    return (y_prompt.reshape(n_ctx, l_ctx, d), y_sample.reshape(n_lat, l_lat, d), new_f, new_b)
```

```python
import functools

import jax
import jax.numpy as jnp
import numpy as np
from jax import lax
from jax.experimental import pallas as pl
from jax.experimental.pallas import tpu as pltpu

D_MODEL = 1024
GRID_W = 64
N_HEADS = 4
HEAD_DIM = 128
RET_WIDTH = N_HEADS * HEAD_DIM
MIX_WIDTH = 2 * RET_WIDTH
POOL_WINDOWS = (2, 4, 8, 16)
POOL_HALO = 8
D_FF = 2816
ROPE_BASE = 10000.0
N_MOD = 9
EPS = 1e-6

FF_COLS = 256
SUB_TILE = 512
FFN_SUB_TILES = 2
TOKEN_TILE = SUB_TILE * FFN_SUB_TILES
RET_TILE = 1024
RET_CHUNK = 256
COL_STAGE_DEPTH = 4
ROW_STAGE_DEPTH = 2
PANEL_SPLIT = 4
V7X_VMEM_LIMIT = 56 * 1024 * 1024

F32 = jnp.float32
BF16 = jnp.bfloat16


def _silu(x):
    return x * (1.0 / (1.0 + jnp.exp(-x)))


def _rmsnorm(x, g):
    return x * lax.rsqrt(jnp.mean(x * x, axis=-1, keepdims=True) + EPS) * g


def _dot(a, b):
    return jnp.dot(a, b, preferred_element_type=F32)


def _dot_tn(a, b):
    return lax.dot_general(a, b, (((0,), (0,)), ((), ())), preferred_element_type=F32)


def _dot_nt(a, b):
    return lax.dot_general(a, b, (((1,), (1,)), ((), ())), preferred_element_type=F32)


def _head(h):
    return slice(h * HEAD_DIM, (h + 1) * HEAD_DIM)


def _resident(shape):
    nd = len(shape)
    return pl.BlockSpec(shape, lambda *_: (0,) * nd, pipeline_mode=pl.Buffered(1))


def _hbm():
    return pl.BlockSpec(memory_space=pl.ANY)


def _mods_kernel(cond_ref, w_ref, b_ref, o_ref):
    s = _silu(cond_ref[...]).astype(BF16)
    o_ref[...] = _dot(s, w_ref[...].astype(BF16)) + b_ref[...]


def _mods(cond, ada_w, ada_b):
    n = N_MOD * D_MODEL
    out = pl.pallas_call(
        _mods_kernel,
        out_shape=jax.ShapeDtypeStruct((8, n), F32),
        grid=(N_MOD,),
        in_specs=[
            pl.BlockSpec((8, D_MODEL), lambda j: (0, 0)),
            pl.BlockSpec((D_MODEL, D_MODEL), lambda j: (0, j)),
            pl.BlockSpec((1, D_MODEL), lambda j: (0, j)),
        ],
        out_specs=pl.BlockSpec((8, D_MODEL), lambda j: (0, j)),
        name="adaln_mods",
    )(cond, ada_w, ada_b.reshape(1, n))
    return out.reshape(8, N_MOD, D_MODEL)


def _swiglu(a_bf16, w1_ref, w3_ref, w2_ref, act_ref):
    for c in range(D_FF // FF_COLS):
        cols = slice(c * FF_COLS, (c + 1) * FF_COLS)
        g = _dot(a_bf16, w1_ref[:, cols])
        u = _dot(a_bf16, w3_ref[:, cols])
        act_ref[:, cols] = (_silu(g) * u).astype(BF16)
    return _dot(act_ref[...], w2_ref[...])


def _panel_stream(srcs, dsts, stage_ref, sem_ref):
    depth, rows = stage_ref.shape[0], stage_ref.shape[1]
    part = rows // PANEL_SPLIT

    def copies(k):
        slot = k % depth
        return [pltpu.make_async_copy(srcs[k].at[pl.ds(s * part, part), :],
                                      stage_ref.at[slot, pl.ds(s * part, part), :], sem_ref.at[slot])
                for s in range(PANEL_SPLIT)]

    def start(k):
        for cp in copies(k):
            cp.start()

    def prime():
        for k in range(min(depth, len(srcs))):
            start(k)

    def take(k):
        for cp in copies(k):
            cp.wait()
        dsts[k][...] = stage_ref[k % depth].astype(BF16)
        if k + depth < len(srcs):
            start(k + depth)

    return prime, take


def _col_panels(ref):
    return [ref.at[:, pl.ds(p * FF_COLS, FF_COLS)] for p in range(ref.shape[1] // FF_COLS)]


def _row_panels(ref):
    return [ref.at[pl.ds(p * FF_COLS, FF_COLS), :] for p in range(ref.shape[0] // FF_COLS)]


def _load_weights(col_pairs, row_pairs, col_stage, col_sem, row_stage, row_sem):
    col_src = [p for src, _ in col_pairs for p in _col_panels(src)]
    col_dst = [p for _, dst in col_pairs for p in _col_panels(dst)]
    row_src = [p for src, _ in row_pairs for p in _row_panels(src)]
    row_dst = [p for _, dst in row_pairs for p in _row_panels(dst)]
    col_prime, col_take = _panel_stream(col_src, col_dst, col_stage, col_sem)
    row_prime, row_take = _panel_stream(row_src, row_dst, row_stage, row_sem)
    col_prime()
    row_prime()
    n_col, n_row = len(col_src), len(row_src)
    for k in range(max(n_col, n_row)):
        for kc in range(k * n_col // max(n_col, n_row), (k + 1) * n_col // max(n_col, n_row)):
            col_take(kc)
        for kr in range(k * n_row // max(n_col, n_row), (k + 1) * n_row // max(n_col, n_row)):
            row_take(kr)


def _export_copies(vmem_refs, hbm_refs, sem_ref):
    return [pltpu.make_async_copy(src, dst, sem_ref.at[k]) for k, (src, dst) in enumerate(zip(vmem_refs, hbm_refs))]


def _converting_call(compute, hbm_weights, col_major, vmem_weights, exports, stage_refs, export_sem):
    step = pl.program_id(0)
    pairs = list(zip(hbm_weights, vmem_weights))

    @pl.when(step == 0)
    def _():
        _load_weights([p for p, c in zip(pairs, col_major) if c], [p for p, c in zip(pairs, col_major) if not c],
                      *stage_refs)
        for cp in _export_copies(vmem_weights, exports, export_sem):
            cp.start()

    @pl.when(step > 0)
    def _():
        compute()

    @pl.when(step == pl.num_programs(0) - 1)
    def _():
        for cp in _export_copies(vmem_weights, exports, export_sem):
            cp.wait()


def _stage_a_compute(x_ref, mods_ref, n1_ref, nmix_ref, h1_ref, a2_ref, act_ref, w1_ref, w3_ref, w2_ref):
    mods = mods_ref[0]
    sh1, sc1, g1, sh2, sc2 = mods[0:1], mods[1:2], mods[2:3], mods[3:4], mods[4:5]
    for s in range(FFN_SUB_TILES):
        rows = slice(s * SUB_TILE, (s + 1) * SUB_TILE)
        x = x_ref[rows, :]
        a1 = _rmsnorm(x, n1_ref[...]) * (1.0 + sc1) + sh1
        h1 = x + (0.5 * g1) * _swiglu(a1.astype(BF16), w1_ref, w3_ref, w2_ref, act_ref.at[s])
        h1_ref[rows, :] = h1
        a2_ref[rows, :] = (_rmsnorm(h1, nmix_ref[...]) * (1.0 + sc2) + sh2).astype(BF16)


def _stage_a_converting_kernel(x_ref, mods_ref, n1_ref, w1_hbm, w3_hbm, w2_hbm, nmix_ref,
                               h1_ref, a2_ref, w1_out, w3_out, w2_out,
                               act_ref, w1_ref, w3_ref, w2_ref, col_stage, col_sem, row_stage, row_sem, export_sem):
    compute = functools.partial(_stage_a_compute, x_ref, mods_ref, n1_ref, nmix_ref, h1_ref, a2_ref, act_ref,
                                w1_ref, w3_ref, w2_ref)
    _converting_call(compute, (w1_hbm, w3_hbm, w2_hbm), (True, True, False), (w1_ref, w3_ref, w2_ref),
                     (w1_out, w3_out, w2_out), (col_stage, col_sem, row_stage, row_sem), export_sem)


def _stage_a_kernel(x_ref, mods_ref, n1_ref, w1_ref, w3_ref, w2_ref, nmix_ref, h1_ref, a2_ref, act_ref):
    _stage_a_compute(x_ref, mods_ref, n1_ref, nmix_ref, h1_ref, a2_ref, act_ref, w1_ref, w3_ref, w2_ref)


def _stage_c_compute(h1_ref, mixin_ref, mods_ref, n2_ref, nf_ref, y_ref, act_ref,
                     wout_ref, w1_ref, w3_ref, w2_ref):
    mods = mods_ref[0]
    g2, sh3, sc3, g3 = mods[5:6], mods[6:7], mods[7:8], mods[8:9]
    for s in range(FFN_SUB_TILES):
        rows = slice(s * SUB_TILE, (s + 1) * SUB_TILE)
        h2 = h1_ref[rows, :] + g2 * _dot(mixin_ref[rows, :], wout_ref[...])
        a3 = _rmsnorm(h2, n2_ref[...]) * (1.0 + sc3) + sh3
        h3 = h2 + (0.5 * g3) * _swiglu(a3.astype(BF16), w1_ref, w3_ref, w2_ref, act_ref.at[s])
        y_ref[rows, :] = _rmsnorm(h3, nf_ref[...])


def _stage_c_converting_kernel(h1_ref, mixin_ref, mods_ref, wout_hbm, n2_ref, w1_hbm, w3_hbm, w2_hbm, nf_ref,
                               y_ref, wout_out, w1_out, w3_out, w2_out,
                               act_ref, w1_ref, w3_ref, w2_ref, col_stage, col_sem, row_stage, row_sem, export_sem,
                               wout_ref):
    compute = functools.partial(_stage_c_compute, h1_ref, mixin_ref, mods_ref, n2_ref, nf_ref, y_ref, act_ref,
                                wout_ref, w1_ref, w3_ref, w2_ref)
    _converting_call(compute, (wout_hbm, w1_hbm, w3_hbm, w2_hbm), (True, True, True, False),
                     (wout_ref, w1_ref, w3_ref, w2_ref), (wout_out, w1_out, w3_out, w2_out),
                     (col_stage, col_sem, row_stage, row_sem), export_sem)


def _stage_c_kernel(h1_ref, mixin_ref, mods_ref, wout_ref, n2_ref, w1_ref, w3_ref, w2_ref, nf_ref,
                    y_ref, act_ref):
    _stage_c_compute(h1_ref, mixin_ref, mods_ref, n2_ref, nf_ref, y_ref, act_ref,
                     wout_ref, w1_ref, w3_ref, w2_ref)


def _ffn_stage_specs(weight, mod_row0, rows_per_mod):
    converting = weight.dtype == F32
    lead = 1 if converting else 0
    tile_of = lambda i: jnp.maximum(i - lead, 0)
    tile = lambda width: pl.BlockSpec((TOKEN_TILE, width), lambda i: (tile_of(i), 0))
    tiles_per_mod = rows_per_mod // TOKEN_TILE
    mods_spec = pl.BlockSpec((1, N_MOD, D_MODEL), lambda i: (mod_row0 + tile_of(i) // tiles_per_mod, 0, 0))
    weight_spec = (lambda w: _hbm()) if converting else (lambda w: _resident(w.shape))
    return tile, mods_spec, weight_spec, converting, lead


def _act_scratch():
    return [pltpu.VMEM((FFN_SUB_TILES, SUB_TILE, D_FF), BF16)]


def _conversion_scratch(weights):
    return ([pltpu.VMEM(w.shape, BF16) for w in weights]
            + [pltpu.VMEM((COL_STAGE_DEPTH, D_MODEL, FF_COLS), F32), pltpu.SemaphoreType.DMA((COL_STAGE_DEPTH,)),
               pltpu.VMEM((ROW_STAGE_DEPTH, FF_COLS, D_MODEL), F32), pltpu.SemaphoreType.DMA((ROW_STAGE_DEPTH,)),
               pltpu.SemaphoreType.DMA((len(weights),))])


def _stage_a(x, mods, mod_row0, rows_per_mod, n1, w1, w3, w2, nmix, name):
    tile, mods_spec, weight_spec, converting, lead = _ffn_stage_specs(w1, mod_row0, rows_per_mod)
    tokens, d, tm = x.shape[0], D_MODEL, TOKEN_TILE
    out_shape = [jax.ShapeDtypeStruct((tokens, d), F32), jax.ShapeDtypeStruct((tokens, d), BF16)]
    out_specs = [tile(d), tile(d)]
    scratch = _act_scratch()
    if converting:
        out_shape += [jax.ShapeDtypeStruct(w.shape, BF16) for w in (w1, w3, w2)]
        out_specs += [_hbm()] * 3
        scratch += _conversion_scratch((w1, w3, w2))
    outs = pl.pallas_call(
        _stage_a_converting_kernel if converting else _stage_a_kernel,
        out_shape=out_shape,
        grid=(lead + tokens // tm,),
        in_specs=[tile(d), mods_spec, _resident((1, d)), weight_spec(w1), weight_spec(w3), weight_spec(w2),
                  _resident((1, d))],
        out_specs=out_specs,
        scratch_shapes=scratch,
        compiler_params=pltpu.CompilerParams(
            dimension_semantics=("arbitrary",), vmem_limit_bytes=V7X_VMEM_LIMIT),
        name=name,
    )(x, mods, n1, w1, w3, w2, nmix)
    return outs[0], outs[1], tuple(outs[2:])


def _stage_c(h1, mixin, mods, mod_row0, rows_per_mod, w_out, n2, w1, w3, w2, nf, name):
    tile, mods_spec, weight_spec, converting, lead = _ffn_stage_specs(w1, mod_row0, rows_per_mod)
    tokens, d, tm = h1.shape[0], D_MODEL, TOKEN_TILE
    out_shape = [jax.ShapeDtypeStruct((tokens, d), F32)]
    out_specs = [tile(d)]
    scratch = _act_scratch()
    if converting:
        out_shape += [jax.ShapeDtypeStruct(w.shape, BF16) for w in (w_out, w1, w3, w2)]
        out_specs += [_hbm()] * 4
        conv = _conversion_scratch((w1, w3, w2))
        conv[-1] = pltpu.SemaphoreType.DMA((4,))
        scratch += conv + [pltpu.VMEM(w_out.shape, BF16)]
    outs = pl.pallas_call(
        _stage_c_converting_kernel if converting else _stage_c_kernel,
        out_shape=out_shape,
        grid=(lead + tokens // tm,),
        in_specs=[tile(d), tile(MIX_WIDTH), mods_spec,
                  weight_spec(w_out), _resident((1, d)), weight_spec(w1), weight_spec(w3), weight_spec(w2),
                  _resident((1, d))],
        out_specs=out_specs,
        scratch_shapes=scratch,
        compiler_params=pltpu.CompilerParams(
            dimension_semantics=("arbitrary",), vmem_limit_bytes=V7X_VMEM_LIMIT),
        name=name,
    )(h1, mixin, mods, w_out, n2, w1, w3, w2, nf)
    return outs[0], tuple(outs[1:])


def _ret_tables(dec_f_ref, dec_b_ref, dmat_ref, qd_ref, kd_ref, cd_ref):
    c = RET_CHUNK
    row = lax.broadcasted_iota(jnp.int32, (c, c), 0)
    col = lax.broadcasted_iota(jnp.int32, (c, c), 1)
    rel = (row - col).astype(F32)
    r = row[:, :HEAD_DIM].astype(F32)
    for h in range(N_HEADS):
        lg_f = -jnp.exp(jnp.broadcast_to(dec_f_ref[0:1, h:h + 1], (1, c)))
        lg_b = -jnp.exp(jnp.broadcast_to(dec_b_ref[0:1, h:h + 1], (1, c)))
        dmat_ref[h] = (jnp.where(rel >= 0, jnp.exp(lg_f * jnp.maximum(rel, 0.0)), 0.0)
                       + jnp.where(rel <= 0, jnp.exp(lg_b * jnp.maximum(-rel, 0.0)), 0.0))
        lf, lb = lg_f[:, :HEAD_DIM], lg_b[:, :HEAD_DIM]
        qd_ref[h] = jnp.concatenate([jnp.exp(lf * (r + 1.0)), jnp.exp(lb * (c - r))], axis=1)
        kd_ref[h] = jnp.concatenate([jnp.exp(lf * (c - 1.0 - r)), jnp.exp(lb * r)], axis=1)
        cd_ref[h] = jnp.concatenate([jnp.exp(lf * c), jnp.exp(lb * c)], axis=1)


def _ret_table_scratch():
    c = RET_CHUNK
    return [pltpu.VMEM((N_HEADS, c, c), F32), pltpu.VMEM((N_HEADS, c, 2 * HEAD_DIM), F32),
            pltpu.VMEM((N_HEADS, c, 2 * HEAD_DIM), F32), pltpu.VMEM((N_HEADS, 1, 2 * HEAD_DIM), F32)]


def _project(a2, win_ref, part, rot):
    t = _dot(a2, win_ref[:, part * RET_WIDTH:(part + 1) * RET_WIDTH])
    if part == 1:
        t = t * (HEAD_DIM ** -0.5)
    if part < 2 and rot is not None:
        cos2, sin2 = rot
        t = jnp.concatenate([t[:, _head(h)] * cos2 + pltpu.roll(t[:, _head(h)], HEAD_DIM // 2, axis=1) * sin2
                             for h in range(N_HEADS)], axis=-1)
    return t


def _chunk_kv(k_c, v_c, kd):
    v32 = v_c.astype(F32)
    vd = (jnp.concatenate([v32, v32], axis=1) * kd).astype(BF16)
    return _dot_tn(k_c, vd)


def _ret_out(q_c, k_c, v_c, dmat, cross, gate_c, gn):
    p = (_dot_nt(q_c, k_c) * dmat).astype(BF16)
    o = _dot(p, v_c)
    if cross is not None:
        o = o + cross[:, :HEAD_DIM] + cross[:, HEAD_DIM:]
    o = o * lax.rsqrt(jnp.mean(o * o, axis=-1, keepdims=True) + EPS) * gn
    return (o * _silu(gate_c)).astype(BF16)


def _pool_inv_count(t0, rows, seq_len, window):
    def edge(start):
        t = start + lax.broadcasted_iota(jnp.int32, (POOL_HALO, HEAD_DIM), 0)
        cnt = jnp.minimum(t + window // 2, seq_len) - jnp.maximum(t - window // 2, 0)
        return 1.0 / cnt.astype(F32)
    inner = jnp.full((rows - 2 * POOL_HALO, HEAD_DIM), 1.0 / window, F32)
    return jnp.concatenate([edge(t0), inner, edge(t0 + rows - POOL_HALO)], axis=0)


def _pool_centred(ext, t0, rows, seq_len, window):
    n_ext = rows + 2 * POOL_HALO
    half = window // 2
    acc, span = ext, 1
    while span < half:
        acc = acc + pltpu.roll(acc, n_ext - span, axis=0)
        span *= 2
    acc = acc + pltpu.roll(acc, half, axis=0)
    win = acc[POOL_HALO:POOL_HALO + rows]
    tok = ext[POOL_HALO:POOL_HALO + rows]
    return win * _pool_inv_count(t0, rows, seq_len, window) - tok


def _pool_groups(ext_of_group, pw_ref, ps_ref, t0, rows, seq_len, store):
    for g, window in enumerate(POOL_WINDOWS):
        centred = _pool_centred(ext_of_group(g), t0, rows, seq_len, window)
        store(g, (_dot(centred.astype(BF16), pw_ref[g].astype(BF16)) * ps_ref[:, _head(g)]).astype(BF16))


def _stage_b_context_kernel(a2_ref, win_ref, dec_f_ref, dec_b_ref, gn_ref, pw_ref, ps_ref,
                            mix_ref, sf_ref, sb_ref, dmat_ref, qd_ref, kd_ref, cd_ref, ext_ref,
                            *, seq_len):
    @pl.when(pl.program_id(0) == 0)
    def _():
        _ret_tables(dec_f_ref, dec_b_ref, dmat_ref, qd_ref, kd_ref, cd_ref)
        zeros = jnp.zeros((POOL_HALO, RET_WIDTH), F32)
        for s in range(ext_ref.shape[0]):
            ext_ref[s, 0:POOL_HALO, :] = zeros
            ext_ref[s, POOL_HALO + seq_len:, :] = zeros

    a2 = a2_ref[...]
    q, k, v = (_project(a2, win_ref, part, None).astype(BF16) for part in range(3))
    gate, u = _project(a2, win_ref, 3, None), _project(a2, win_ref, 4, None)
    for s in range(a2_ref.shape[0] // seq_len):
        rows = slice(s * seq_len, (s + 1) * seq_len)
        for h in range(N_HEADS):
            q_c, k_c, v_c = q[rows, _head(h)], k[rows, _head(h)], v[rows, _head(h)]
            mix_ref[rows, _head(h)] = _ret_out(q_c, k_c, v_c, dmat_ref[h], None, gate[rows, _head(h)],
                                               gn_ref[:, _head(h)])
            kv = _chunk_kv(k_c, v_c, kd_ref[h])
            sf_ref[s, 0, h] = kv[:, :HEAD_DIM]
            sb_ref[s, 0, h] = kv[:, HEAD_DIM:]
        ext_ref[s, POOL_HALO:POOL_HALO + seq_len, :] = u[rows, :]

        def store(g, val, rows=rows):
            mix_ref[rows, _head(N_HEADS + g)] = val
        _pool_groups(lambda g, s=s: ext_ref[s, :, _head(g)], pw_ref, ps_ref, 0, seq_len, seq_len, store)


def _stage_b_context(a2, w_in, dec_f, dec_b, gn, pool_w, pool_scale, n_seq, seq_len):
    assert seq_len == RET_CHUNK
    tokens = n_seq * seq_len
    tl = RET_TILE
    seqs = tl // seq_len
    tile = lambda width: pl.BlockSpec((tl, width), lambda i: (i, 0))
    state_shape = jax.ShapeDtypeStruct((n_seq, 1, N_HEADS, HEAD_DIM, HEAD_DIM), F32)
    state_spec = lambda: pl.BlockSpec((seqs, 1, N_HEADS, HEAD_DIM, HEAD_DIM), lambda i: (i, 0, 0, 0, 0))
    return pl.pallas_call(
        functools.partial(_stage_b_context_kernel, seq_len=seq_len),
        out_shape=[jax.ShapeDtypeStruct((tokens, MIX_WIDTH), BF16)] + [state_shape] * 2,
        grid=(tokens // tl,),
        in_specs=[tile(D_MODEL), _resident(w_in.shape),
                  _resident(dec_f.shape), _resident(dec_b.shape), _resident(gn.shape),
                  _resident(pool_w.shape), _resident(pool_scale.shape)],
        out_specs=[tile(MIX_WIDTH), state_spec(), state_spec()],
        scratch_shapes=_ret_table_scratch() + [pltpu.VMEM((seqs, seq_len + 2 * POOL_HALO, RET_WIDTH), F32)],
        compiler_params=pltpu.CompilerParams(
            dimension_semantics=("arbitrary",), vmem_limit_bytes=V7X_VMEM_LIMIT),
        name="stage_b_context",
    )(a2, w_in, dec_f, dec_b, gn, pool_w, pool_scale)


def _stage_b_latent_kernel(a2_ref, cos_ref, sin_ref, win_ref, dec_f_ref, dec_b_ref, gn_ref, pw_ref, ps_ref,
                           s0f_ref, s0b_ref,
                           mix_ref,
                           dmat_ref, qd_ref, kd_ref, cd_ref, sf_ref, sb_ref, sb_hist_ref, kvf_hist_ref,
                           kseq_ref, vseq_ref, useq_ref,
                           *, seq_len, n_tiles):
    c = RET_CHUNK
    tl = a2_ref.shape[1]
    chunks = tl // c
    j = pl.program_id(1)
    rot = (cos_ref[...], sin_ref[...])

    @pl.when((pl.program_id(0) == 0) & (j == 0))
    def _():
        _ret_tables(dec_f_ref, dec_b_ref, dmat_ref, qd_ref, kd_ref, cd_ref)
        zeros = jnp.zeros((POOL_HALO, RET_WIDTH), F32)
        useq_ref[0:POOL_HALO, :] = zeros
        useq_ref[POOL_HALO + seq_len:, :] = zeros

    @pl.when(j == 0)
    def _():
        for h in range(N_HEADS):
            sb_ref[h] = s0b_ref[0, h]
            sf_ref[h] = s0f_ref[0, h]

    @pl.when(j < n_tiles)
    def _():
        tile = n_tiles - 1 - j
        t0 = pl.multiple_of(tile * tl, tl)
        a2 = a2_ref[0]
        k = _project(a2, win_ref, 1, rot).astype(BF16)
        v = _project(a2, win_ref, 2, rot).astype(BF16)
        kseq_ref[pl.ds(t0, tl), :] = k
        vseq_ref[pl.ds(t0, tl), :] = v
        useq_ref[pl.ds(t0 + POOL_HALO, tl), :] = _project(a2, win_ref, 4, rot)
        for ci in reversed(range(chunks)):
            n = tile * chunks + ci
            rows = slice(ci * c, (ci + 1) * c)
            for h in range(N_HEADS):
                kv = _chunk_kv(k[rows, _head(h)], v[rows, _head(h)], kd_ref[h])
                s_b = sb_ref[h]
                sb_hist_ref[n, h] = s_b.astype(BF16)
                kvf_hist_ref[n, h] = kv[:, :HEAD_DIM]
                sb_ref[h] = s_b * cd_ref[h][:, HEAD_DIM:] + kv[:, HEAD_DIM:]

    @pl.when(j >= n_tiles)
    def _():
        tile = j - n_tiles
        t0 = pl.multiple_of(tile * tl, tl)
        a2 = a2_ref[0]
        q = _project(a2, win_ref, 0, rot).astype(BF16)
        gate = _project(a2, win_ref, 3, rot)
        for ci in range(chunks):
            n = tile * chunks + ci
            rows = slice(ci * c, (ci + 1) * c)
            seq_rows = pl.ds(t0 + ci * c, c)
            for h in range(N_HEADS):
                q_c = q[rows, _head(h)]
                s_f = sf_ref[h]
                s2 = jnp.concatenate([s_f.astype(BF16), sb_hist_ref[n, h]], axis=1)
                cross = _dot(q_c, s2) * qd_ref[h]
                mix_ref[0, rows, _head(h)] = _ret_out(q_c, kseq_ref[seq_rows, _head(h)], vseq_ref[seq_rows, _head(h)],
                                                      dmat_ref[h], cross, gate[rows, _head(h)], gn_ref[:, _head(h)])
                sf_ref[h] = s_f * cd_ref[h][:, :HEAD_DIM] + kvf_hist_ref[n, h]

        def store(g, val):
            mix_ref[0, :, _head(N_HEADS + g)] = val
        _pool_groups(lambda g: useq_ref[pl.ds(t0, tl + 2 * POOL_HALO), _head(g)], pw_ref, ps_ref,
                     tile * tl, tl, seq_len, store)


def _stage_b_latent(a2, rot, w_in, dec_f, dec_b, gn, pool_w, pool_scale, s0f, s0b):
    b, l, d = a2.shape
    tl = RET_TILE
    nt = l // tl
    n_chunks = l // RET_CHUNK
    both_tile = lambda j: jnp.where(j < nt, nt - 1 - j, j - nt)
    fwd = lambda: pl.BlockSpec((1, tl, MIX_WIDTH), lambda i, j: (i, jnp.maximum(j - nt, 0), 0))
    rot_spec = lambda: pl.BlockSpec((tl, HEAD_DIM), lambda i, j: (both_tile(j), 0))
    state = lambda: pl.BlockSpec((1, N_HEADS, HEAD_DIM, HEAD_DIM), lambda i, j: (i, 0, 0, 0))
    return pl.pallas_call(
        functools.partial(_stage_b_latent_kernel, seq_len=l, n_tiles=nt),
        out_shape=jax.ShapeDtypeStruct((b, l, MIX_WIDTH), BF16),
        grid=(b, 2 * nt),
        in_specs=[pl.BlockSpec((1, tl, d), lambda i, j: (i, both_tile(j), 0)), rot_spec(), rot_spec(),
                  _resident(w_in.shape), _resident(dec_f.shape), _resident(dec_b.shape), _resident(gn.shape),
                  _resident(pool_w.shape), _resident(pool_scale.shape), state(), state()],
        out_specs=fwd(),
        scratch_shapes=_ret_table_scratch() + [
            pltpu.VMEM((N_HEADS, HEAD_DIM, HEAD_DIM), F32), pltpu.VMEM((N_HEADS, HEAD_DIM, HEAD_DIM), F32),
            pltpu.VMEM((n_chunks, N_HEADS, HEAD_DIM, HEAD_DIM), BF16),
            pltpu.VMEM((n_chunks, N_HEADS, HEAD_DIM, HEAD_DIM), F32),
            pltpu.VMEM((l, RET_WIDTH), BF16), pltpu.VMEM((l, RET_WIDTH), BF16),
            pltpu.VMEM((l + 2 * POOL_HALO, RET_WIDTH), F32)],
        compiler_params=pltpu.CompilerParams(
            dimension_semantics=("arbitrary", "arbitrary"), vmem_limit_bytes=V7X_VMEM_LIMIT),
        name="stage_b_latent",
    )(a2, *rot, w_in, dec_f, dec_b, gn, pool_w, pool_scale, s0f, s0b)


def _rotary_tables(seq_len):
    rows = seq_len // GRID_W
    row = np.repeat(np.arange(rows, dtype=np.float64), GRID_W)
    col = np.tile(np.arange(GRID_W, dtype=np.float64), rows)
    n_half = HEAD_DIM // 4
    freqs = ROPE_BASE ** (-np.arange(n_half, dtype=np.float64) / n_half)
    ang = np.concatenate([row[:, None] * freqs, col[:, None] * freqs], axis=-1)
    cos, sin = np.cos(ang), np.sin(ang)
    cos2 = np.concatenate([cos, cos], axis=-1).astype(np.float32)
    sin2 = np.concatenate([-sin, sin], axis=-1).astype(np.float32)
    return jnp.asarray(cos2), jnp.asarray(sin2)


def kernel(x_prompt, x_sample, state_ret_fwd, state_ret_bwd, c, c_ctx, ada_w, ada_b, norm_ffn1, ffn1_w1, ffn1_w3, ffn1_w2, norm_mix, w_in, ret_decay_fwd, ret_decay_bwd, ret_gn, pool_w, pool_scale, w_out, norm_ffn2, ffn2_w1, ffn2_w3, ffn2_w2, norm_final):
    depth = ada_w.shape[0]
    assert depth == 1, "single trunk layer"
    d = D_MODEL
    n_ctx, l_ctx, _ = x_prompt.shape
    n_lat, l_lat, _ = x_sample.shape
    t_ctx, t_lat = n_ctx * l_ctx, n_lat * l_lat

    cond = jnp.concatenate([c_ctx[None, :], c, jnp.zeros((8 - 1 - n_lat, d), F32)], axis=0)
    mods = _mods(cond, ada_w[0], ada_b[0])

    row = lambda g: g.reshape(1, -1)
    w_in_b = w_in[0].astype(BF16)
    n1, nmix, n2, nf = row(norm_ffn1[0]), row(norm_mix[0]), row(norm_ffn2[0]), row(norm_final)
    dec_f, dec_b = ret_decay_fwd[0].reshape(1, N_HEADS), ret_decay_bwd[0].reshape(1, N_HEADS)
    gn, ps = row(ret_gn[0]), row(pool_scale[0])

    h1c, a2c, ffn1_b = _stage_a(x_prompt.reshape(t_ctx, d), mods, 0, t_ctx, n1, ffn1_w1[0], ffn1_w3[0], ffn1_w2[0],
                                nmix, "stage_a_context")
    mixc, new_f, new_b = _stage_b_context(a2c, w_in_b, dec_f, dec_b, gn, pool_w[0], ps, n_ctx, l_ctx)
    y_prompt, (w_out_b, *ffn2_b) = _stage_c(h1c, mixc, mods, 0, t_ctx, w_out[0], n2,
                                           ffn2_w1[0], ffn2_w3[0], ffn2_w2[0], nf, "stage_c_context")

    h1l, a2l, _ = _stage_a(x_sample.reshape(t_lat, d), mods, 1, l_lat, n1, *ffn1_b, nmix, "stage_a_latent")
    mixl = _stage_b_latent(a2l.reshape(n_lat, l_lat, d), _rotary_tables(l_lat), w_in_b, dec_f, dec_b, gn,
                           pool_w[0], ps, state_ret_fwd[:, 0], state_ret_bwd[:, 0])
    y_sample, _ = _stage_c(h1l, mixl.reshape(t_lat, MIX_WIDTH), mods, 1, l_lat,
                           w_out_b, n2, *ffn2_b, nf, "stage_c_latent")

    return (y_prompt.reshape(n_ctx, l_ctx, d), y_sample.reshape(n_lat, l_lat, d), new_f, new_b)
```

```python
import functools

import jax
import jax.numpy as jnp
import numpy as np
from jax import lax
from jax.experimental import pallas as pl
from jax.experimental.pallas import tpu as pltpu

D_MODEL = 1024
GRID_W = 64
N_HEADS = 4
HEAD_DIM = 128
RET_WIDTH = N_HEADS * HEAD_DIM
MIX_WIDTH = 2 * RET_WIDTH
POOL_WINDOWS = (2, 4, 8, 16)
POOL_HALO = 8
D_FF = 2816
ROPE_BASE = 10000.0
N_MOD = 9
EPS = 1e-6

MODS_COLS = 3072
FF_COLS = 256
SUB_TILE = 512
FFN_SUB_TILES = 2
TOKEN_TILE = SUB_TILE * FFN_SUB_TILES
RET_TILE = 1024
RET_CHUNK = 256
COL_STAGE_DEPTH = 4
ROW_STAGE_DEPTH = 2
PANEL_SPLIT = 4
V7X_VMEM_LIMIT = 56 * 1024 * 1024

F32 = jnp.float32
BF16 = jnp.bfloat16


def _silu(x):
    return x * (1.0 / (1.0 + jnp.exp(-x)))


def _rmsnorm(x, g):
    return x * lax.rsqrt(jnp.mean(x * x, axis=-1, keepdims=True) + EPS) * g


def _dot(a, b):
    return jnp.dot(a, b, preferred_element_type=F32)


def _dot_tn(a, b):
    return lax.dot_general(a, b, (((0,), (0,)), ((), ())), preferred_element_type=F32)


def _dot_nt(a, b):
    return lax.dot_general(a, b, (((1,), (1,)), ((), ())), preferred_element_type=F32)


def _head(h):
    return slice(h * HEAD_DIM, (h + 1) * HEAD_DIM)


def _resident(shape):
    nd = len(shape)
    return pl.BlockSpec(shape, lambda *_: (0,) * nd, pipeline_mode=pl.Buffered(1))


def _hbm():
    return pl.BlockSpec(memory_space=pl.ANY)


def _mods_kernel(cond_ref, w_ref, b_ref, o_ref):
    s = _silu(cond_ref[...]).astype(BF16)
    o_ref[...] = _dot(s, w_ref[...].astype(BF16)) + b_ref[...]


def _mods(cond, ada_w, ada_b):
    n = N_MOD * D_MODEL
    cols = MODS_COLS
    out = pl.pallas_call(
        _mods_kernel,
        out_shape=jax.ShapeDtypeStruct((8, n), F32),
        grid=(n // cols,),
        in_specs=[
            pl.BlockSpec((8, D_MODEL), lambda j: (0, 0)),
            pl.BlockSpec((D_MODEL, cols), lambda j: (0, j)),
            pl.BlockSpec((1, cols), lambda j: (0, j)),
        ],
        out_specs=pl.BlockSpec((8, cols), lambda j: (0, j)),
        compiler_params=pltpu.CompilerParams(
            dimension_semantics=("arbitrary",), vmem_limit_bytes=V7X_VMEM_LIMIT),
        name="adaln_mods",
    )(cond, ada_w, ada_b.reshape(1, n))
    return out.reshape(8, N_MOD, D_MODEL)


def _swiglu(a_bf16, w1_ref, w3_ref, w2_ref, act_ref):
    for c in range(D_FF // FF_COLS):
        cols = slice(c * FF_COLS, (c + 1) * FF_COLS)
        g = _dot(a_bf16, w1_ref[:, cols])
        u = _dot(a_bf16, w3_ref[:, cols])
        act_ref[:, cols] = (_silu(g) * u).astype(BF16)
    return _dot(act_ref[...], w2_ref[...])


def _panel_stream(srcs, dsts, stage_ref, sem_ref):
    depth, rows = stage_ref.shape[0], stage_ref.shape[1]
    part = rows // PANEL_SPLIT

    def copies(k):
        slot = k % depth
        return [pltpu.make_async_copy(srcs[k].at[pl.ds(s * part, part), :],
                                      stage_ref.at[slot, pl.ds(s * part, part), :], sem_ref.at[slot])
                for s in range(PANEL_SPLIT)]

    def start(k):
        for cp in copies(k):
            cp.start()

    def prime():
        for k in range(min(depth, len(srcs))):
            start(k)

    def take(k):
        for cp in copies(k):
            cp.wait()
        dsts[k][...] = stage_ref[k % depth].astype(BF16)
        if k + depth < len(srcs):
            start(k + depth)

    return prime, take


def _col_panels(ref):
    return [ref.at[:, pl.ds(p * FF_COLS, FF_COLS)] for p in range(ref.shape[1] // FF_COLS)]


def _row_panels(ref):
    return [ref.at[pl.ds(p * FF_COLS, FF_COLS), :] for p in range(ref.shape[0] // FF_COLS)]


def _load_weights(col_pairs, row_pairs, col_stage, col_sem, row_stage, row_sem):
    col_src = [p for src, _ in col_pairs for p in _col_panels(src)]
    col_dst = [p for _, dst in col_pairs for p in _col_panels(dst)]
    row_src = [p for src, _ in row_pairs for p in _row_panels(src)]
    row_dst = [p for _, dst in row_pairs for p in _row_panels(dst)]
    col_prime, col_take = _panel_stream(col_src, col_dst, col_stage, col_sem)
    row_prime, row_take = _panel_stream(row_src, row_dst, row_stage, row_sem)
    col_prime()
    row_prime()
    n_col, n_row = len(col_src), len(row_src)
    for k in range(max(n_col, n_row)):
        for kc in range(k * n_col // max(n_col, n_row), (k + 1) * n_col // max(n_col, n_row)):
            col_take(kc)
        for kr in range(k * n_row // max(n_col, n_row), (k + 1) * n_row // max(n_col, n_row)):
            row_take(kr)


def _export_copies(vmem_refs, hbm_refs, sem_ref):
    return [pltpu.make_async_copy(src, dst, sem_ref.at[k]) for k, (src, dst) in enumerate(zip(vmem_refs, hbm_refs))]


def _converting_call(compute, hbm_weights, col_major, vmem_weights, exports, stage_refs, export_sem):
    step = pl.program_id(0)
    pairs = list(zip(hbm_weights, vmem_weights))

    @pl.when(step == 0)
    def _():
        _load_weights([p for p, c in zip(pairs, col_major) if c], [p for p, c in zip(pairs, col_major) if not c],
                      *stage_refs)
        for cp in _export_copies(vmem_weights, exports, export_sem):
            cp.start()

    @pl.when(step > 0)
    def _():
        compute()

    @pl.when(step == pl.num_programs(0) - 1)
    def _():
        for cp in _export_copies(vmem_weights, exports, export_sem):
            cp.wait()


def _stage_a_compute(x_ref, mods_ref, n1_ref, nmix_ref, h1_ref, a2_ref, act_ref, w1_ref, w3_ref, w2_ref):
    mods = mods_ref[0]
    sh1, sc1, g1, sh2, sc2 = mods[0:1], mods[1:2], mods[2:3], mods[3:4], mods[4:5]
    for s in range(FFN_SUB_TILES):
        rows = slice(s * SUB_TILE, (s + 1) * SUB_TILE)
        x = x_ref[rows, :]
        a1 = _rmsnorm(x, n1_ref[...]) * (1.0 + sc1) + sh1
        h1 = x + (0.5 * g1) * _swiglu(a1.astype(BF16), w1_ref, w3_ref, w2_ref, act_ref.at[s])
        h1_ref[rows, :] = h1
        a2_ref[rows, :] = (_rmsnorm(h1, nmix_ref[...]) * (1.0 + sc2) + sh2).astype(BF16)


def _stage_a_converting_kernel(x_ref, mods_ref, n1_ref, w1_hbm, w3_hbm, w2_hbm, nmix_ref,
                               h1_ref, a2_ref, w1_out, w3_out, w2_out,
                               act_ref, w1_ref, w3_ref, w2_ref, col_stage, col_sem, row_stage, row_sem, export_sem):
    compute = functools.partial(_stage_a_compute, x_ref, mods_ref, n1_ref, nmix_ref, h1_ref, a2_ref, act_ref,
                                w1_ref, w3_ref, w2_ref)
    _converting_call(compute, (w1_hbm, w3_hbm, w2_hbm), (True, True, False), (w1_ref, w3_ref, w2_ref),
                     (w1_out, w3_out, w2_out), (col_stage, col_sem, row_stage, row_sem), export_sem)


def _stage_a_kernel(x_ref, mods_ref, n1_ref, w1_ref, w3_ref, w2_ref, nmix_ref, h1_ref, a2_ref, act_ref):
    _stage_a_compute(x_ref, mods_ref, n1_ref, nmix_ref, h1_ref, a2_ref, act_ref, w1_ref, w3_ref, w2_ref)


def _stage_c_compute(h1_ref, mixin_ref, mods_ref, n2_ref, nf_ref, y_ref, act_ref,
                     wout_ref, w1_ref, w3_ref, w2_ref):
    mods = mods_ref[0]
    g2, sh3, sc3, g3 = mods[5:6], mods[6:7], mods[7:8], mods[8:9]
    for s in range(FFN_SUB_TILES):
        rows = slice(s * SUB_TILE, (s + 1) * SUB_TILE)
        h2 = h1_ref[rows, :] + g2 * _dot(mixin_ref[rows, :], wout_ref[...])
        a3 = _rmsnorm(h2, n2_ref[...]) * (1.0 + sc3) + sh3
        h3 = h2 + (0.5 * g3) * _swiglu(a3.astype(BF16), w1_ref, w3_ref, w2_ref, act_ref.at[s])
        y_ref[rows, :] = _rmsnorm(h3, nf_ref[...])


def _stage_c_converting_kernel(h1_ref, mixin_ref, mods_ref, wout_hbm, n2_ref, w1_hbm, w3_hbm, w2_hbm, nf_ref,
                               y_ref, wout_out, w1_out, w3_out, w2_out,
                               act_ref, w1_ref, w3_ref, w2_ref, col_stage, col_sem, row_stage, row_sem, export_sem,
                               wout_ref):
    compute = functools.partial(_stage_c_compute, h1_ref, mixin_ref, mods_ref, n2_ref, nf_ref, y_ref, act_ref,
                                wout_ref, w1_ref, w3_ref, w2_ref)
    _converting_call(compute, (wout_hbm, w1_hbm, w3_hbm, w2_hbm), (True, True, True, False),
                     (wout_ref, w1_ref, w3_ref, w2_ref), (wout_out, w1_out, w3_out, w2_out),
                     (col_stage, col_sem, row_stage, row_sem), export_sem)


def _stage_c_kernel(h1_ref, mixin_ref, mods_ref, wout_ref, n2_ref, w1_ref, w3_ref, w2_ref, nf_ref,
                    y_ref, act_ref):
    _stage_c_compute(h1_ref, mixin_ref, mods_ref, n2_ref, nf_ref, y_ref, act_ref,
                     wout_ref, w1_ref, w3_ref, w2_ref)


def _ffn_stage_specs(weight, mod_row0, rows_per_mod):
    converting = weight.dtype == F32
    lead = 1 if converting else 0
    tile_of = lambda i: jnp.maximum(i - lead, 0)
    tile = lambda width: pl.BlockSpec((TOKEN_TILE, width), lambda i: (tile_of(i), 0))
    tiles_per_mod = rows_per_mod // TOKEN_TILE
    mods_spec = pl.BlockSpec((1, N_MOD, D_MODEL), lambda i: (mod_row0 + tile_of(i) // tiles_per_mod, 0, 0))
    weight_spec = (lambda w: _hbm()) if converting else (lambda w: _resident(w.shape))
    return tile, mods_spec, weight_spec, converting, lead


def _act_scratch():
    return [pltpu.VMEM((FFN_SUB_TILES, SUB_TILE, D_FF), BF16)]


def _conversion_scratch(weights):
    return ([pltpu.VMEM(w.shape, BF16) for w in weights]
            + [pltpu.VMEM((COL_STAGE_DEPTH, D_MODEL, FF_COLS), F32), pltpu.SemaphoreType.DMA((COL_STAGE_DEPTH,)),
               pltpu.VMEM((ROW_STAGE_DEPTH, FF_COLS, D_MODEL), F32), pltpu.SemaphoreType.DMA((ROW_STAGE_DEPTH,)),
               pltpu.SemaphoreType.DMA((len(weights),))])


def _stage_a(x, mods, mod_row0, rows_per_mod, n1, w1, w3, w2, nmix, name):
    tile, mods_spec, weight_spec, converting, lead = _ffn_stage_specs(w1, mod_row0, rows_per_mod)
    tokens, d, tm = x.shape[0], D_MODEL, TOKEN_TILE
    out_shape = [jax.ShapeDtypeStruct((tokens, d), F32), jax.ShapeDtypeStruct((tokens, d), BF16)]
    out_specs = [tile(d), tile(d)]
    scratch = _act_scratch()
    if converting:
        out_shape += [jax.ShapeDtypeStruct(w.shape, BF16) for w in (w1, w3, w2)]
        out_specs += [_hbm()] * 3
        scratch += _conversion_scratch((w1, w3, w2))
    outs = pl.pallas_call(
        _stage_a_converting_kernel if converting else _stage_a_kernel,
        out_shape=out_shape,
        grid=(lead + tokens // tm,),
        in_specs=[tile(d), mods_spec, _resident((1, d)), weight_spec(w1), weight_spec(w3), weight_spec(w2),
                  _resident((1, d))],
        out_specs=out_specs,
        scratch_shapes=scratch,
        compiler_params=pltpu.CompilerParams(
            dimension_semantics=("arbitrary",), vmem_limit_bytes=V7X_VMEM_LIMIT),
        name=name,
    )(x, mods, n1, w1, w3, w2, nmix)
    return outs[0], outs[1], tuple(outs[2:])


def _stage_c(h1, mixin, mods, mod_row0, rows_per_mod, w_out, n2, w1, w3, w2, nf, name):
    tile, mods_spec, weight_spec, converting, lead = _ffn_stage_specs(w1, mod_row0, rows_per_mod)
    tokens, d, tm = h1.shape[0], D_MODEL, TOKEN_TILE
    out_shape = [jax.ShapeDtypeStruct((tokens, d), F32)]
    out_specs = [tile(d)]
    scratch = _act_scratch()
    if converting:
        out_shape += [jax.ShapeDtypeStruct(w.shape, BF16) for w in (w_out, w1, w3, w2)]
        out_specs += [_hbm()] * 4
        conv = _conversion_scratch((w1, w3, w2))
        conv[-1] = pltpu.SemaphoreType.DMA((4,))
        scratch += conv + [pltpu.VMEM(w_out.shape, BF16)]
    outs = pl.pallas_call(
        _stage_c_converting_kernel if converting else _stage_c_kernel,
        out_shape=out_shape,
        grid=(lead + tokens // tm,),
        in_specs=[tile(d), tile(MIX_WIDTH), mods_spec,
                  weight_spec(w_out), _resident((1, d)), weight_spec(w1), weight_spec(w3), weight_spec(w2),
                  _resident((1, d))],
        out_specs=out_specs,
        scratch_shapes=scratch,
        compiler_params=pltpu.CompilerParams(
            dimension_semantics=("arbitrary",), vmem_limit_bytes=V7X_VMEM_LIMIT),
        name=name,
    )(h1, mixin, mods, w_out, n2, w1, w3, w2, nf)
    return outs[0], tuple(outs[1:])


def _ret_tables(dec_f_ref, dec_b_ref, dmat_ref, qd_ref, kd_ref, cd_ref):
    c = RET_CHUNK
    row = lax.broadcasted_iota(jnp.int32, (c, c), 0)
    col = lax.broadcasted_iota(jnp.int32, (c, c), 1)
    rel = (row - col).astype(F32)
    r = row[:, :HEAD_DIM].astype(F32)
    for h in range(N_HEADS):
        lg_f = -jnp.exp(jnp.broadcast_to(dec_f_ref[0:1, h:h + 1], (1, c)))
        lg_b = -jnp.exp(jnp.broadcast_to(dec_b_ref[0:1, h:h + 1], (1, c)))
        dmat_ref[h] = (jnp.where(rel >= 0, jnp.exp(lg_f * jnp.maximum(rel, 0.0)), 0.0)
                       + jnp.where(rel <= 0, jnp.exp(lg_b * jnp.maximum(-rel, 0.0)), 0.0))
        lf, lb = lg_f[:, :HEAD_DIM], lg_b[:, :HEAD_DIM]
        qd_ref[h] = jnp.concatenate([jnp.exp(lf * (r + 1.0)), jnp.exp(lb * (c - r))], axis=1)
        kd_ref[h] = jnp.concatenate([jnp.exp(lf * (c - 1.0 - r)), jnp.exp(lb * r)], axis=1)
        cd_ref[h] = jnp.concatenate([jnp.exp(lf * c), jnp.exp(lb * c)], axis=1)


def _ret_table_scratch():
    c = RET_CHUNK
    return [pltpu.VMEM((N_HEADS, c, c), F32), pltpu.VMEM((N_HEADS, c, 2 * HEAD_DIM), F32),
            pltpu.VMEM((N_HEADS, c, 2 * HEAD_DIM), F32), pltpu.VMEM((N_HEADS, 1, 2 * HEAD_DIM), F32)]


def _project(a2, win_ref, part, rot):
    t = _dot(a2, win_ref[:, part * RET_WIDTH:(part + 1) * RET_WIDTH])
    if part == 1:
        t = t * (HEAD_DIM ** -0.5)
    if part < 2 and rot is not None:
        cos2, sin2 = rot
        t = jnp.concatenate([t[:, _head(h)] * cos2 + pltpu.roll(t[:, _head(h)], HEAD_DIM // 2, axis=1) * sin2
                             for h in range(N_HEADS)], axis=-1)
    return t


def _chunk_kv(k_c, v_c, kd):
    v32 = v_c.astype(F32)
    vd = (jnp.concatenate([v32, v32], axis=1) * kd).astype(BF16)
    return _dot_tn(k_c, vd)


def _ret_out(q_c, k_c, v_c, dmat, cross, gate_c, gn):
    p = (_dot_nt(q_c, k_c) * dmat).astype(BF16)
    o = _dot(p, v_c)
    if cross is not None:
        o = o + cross[:, :HEAD_DIM] + cross[:, HEAD_DIM:]
    o = o * lax.rsqrt(jnp.mean(o * o, axis=-1, keepdims=True) + EPS) * gn
    return (o * _silu(gate_c)).astype(BF16)


def _pool_inv_count(t0, rows, seq_len, window):
    def edge(start):
        t = start + lax.broadcasted_iota(jnp.int32, (POOL_HALO, HEAD_DIM), 0)
        cnt = jnp.minimum(t + window // 2, seq_len) - jnp.maximum(t - window // 2, 0)
        return 1.0 / cnt.astype(F32)
    inner = jnp.full((rows - 2 * POOL_HALO, HEAD_DIM), 1.0 / window, F32)
    return jnp.concatenate([edge(t0), inner, edge(t0 + rows - POOL_HALO)], axis=0)


def _pool_centred(ext, t0, rows, seq_len, window):
    n_ext = rows + 2 * POOL_HALO
    half = window // 2
    acc, span = ext, 1
    while span < half:
        acc = acc + pltpu.roll(acc, n_ext - span, axis=0)
        span *= 2
    acc = acc + pltpu.roll(acc, half, axis=0)
    win = acc[POOL_HALO:POOL_HALO + rows]
    tok = ext[POOL_HALO:POOL_HALO + rows]
    return win * _pool_inv_count(t0, rows, seq_len, window) - tok


def _pool_groups(ext_of_group, pw_ref, ps_ref, t0, rows, seq_len, store):
    for g, window in enumerate(POOL_WINDOWS):
        centred = _pool_centred(ext_of_group(g), t0, rows, seq_len, window)
        store(g, (_dot(centred.astype(BF16), pw_ref[g].astype(BF16)) * ps_ref[:, _head(g)]).astype(BF16))


def _stage_b_context_kernel(a2_ref, win_ref, dec_f_ref, dec_b_ref, gn_ref, pw_ref, ps_ref,
                            mix_ref, sf_ref, sb_ref, dmat_ref, qd_ref, kd_ref, cd_ref, ext_ref,
                            *, seq_len):
    @pl.when(pl.program_id(0) == 0)
    def _():
        _ret_tables(dec_f_ref, dec_b_ref, dmat_ref, qd_ref, kd_ref, cd_ref)
        zeros = jnp.zeros((POOL_HALO, RET_WIDTH), F32)
        for s in range(ext_ref.shape[0]):
            ext_ref[s, 0:POOL_HALO, :] = zeros
            ext_ref[s, POOL_HALO + seq_len:, :] = zeros

    a2 = a2_ref[...]
    q, k, v = (_project(a2, win_ref, part, None).astype(BF16) for part in range(3))
    gate, u = _project(a2, win_ref, 3, None), _project(a2, win_ref, 4, None)
    for s in range(a2_ref.shape[0] // seq_len):
        rows = slice(s * seq_len, (s + 1) * seq_len)
        for h in range(N_HEADS):
            q_c, k_c, v_c = q[rows, _head(h)], k[rows, _head(h)], v[rows, _head(h)]
            mix_ref[rows, _head(h)] = _ret_out(q_c, k_c, v_c, dmat_ref[h], None, gate[rows, _head(h)],
                                               gn_ref[:, _head(h)])
            kv = _chunk_kv(k_c, v_c, kd_ref[h])
            sf_ref[s, 0, h] = kv[:, :HEAD_DIM]
            sb_ref[s, 0, h] = kv[:, HEAD_DIM:]
        ext_ref[s, POOL_HALO:POOL_HALO + seq_len, :] = u[rows, :]

        def store(g, val, rows=rows):
            mix_ref[rows, _head(N_HEADS + g)] = val
        _pool_groups(lambda g, s=s: ext_ref[s, :, _head(g)], pw_ref, ps_ref, 0, seq_len, seq_len, store)


def _stage_b_context(a2, w_in, dec_f, dec_b, gn, pool_w, pool_scale, n_seq, seq_len):
    assert seq_len == RET_CHUNK
    tokens = n_seq * seq_len
    tl = RET_TILE
    seqs = tl // seq_len
    tile = lambda width: pl.BlockSpec((tl, width), lambda i: (i, 0))
    state_shape = jax.ShapeDtypeStruct((n_seq, 1, N_HEADS, HEAD_DIM, HEAD_DIM), F32)
    state_spec = lambda: pl.BlockSpec((seqs, 1, N_HEADS, HEAD_DIM, HEAD_DIM), lambda i: (i, 0, 0, 0, 0))
    return pl.pallas_call(
        functools.partial(_stage_b_context_kernel, seq_len=seq_len),
        out_shape=[jax.ShapeDtypeStruct((tokens, MIX_WIDTH), BF16)] + [state_shape] * 2,
        grid=(tokens // tl,),
        in_specs=[tile(D_MODEL), _resident(w_in.shape),
                  _resident(dec_f.shape), _resident(dec_b.shape), _resident(gn.shape),
                  _resident(pool_w.shape), _resident(pool_scale.shape)],
        out_specs=[tile(MIX_WIDTH), state_spec(), state_spec()],
        scratch_shapes=_ret_table_scratch() + [pltpu.VMEM((seqs, seq_len + 2 * POOL_HALO, RET_WIDTH), F32)],
        compiler_params=pltpu.CompilerParams(
            dimension_semantics=("arbitrary",), vmem_limit_bytes=V7X_VMEM_LIMIT),
        name="stage_b_context",
    )(a2, w_in, dec_f, dec_b, gn, pool_w, pool_scale)


def _stage_b_latent_kernel(a2_ref, cos_ref, sin_ref, win_ref, dec_f_ref, dec_b_ref, gn_ref, pw_ref, ps_ref,
                           s0f_ref, s0b_ref,
                           mix_ref,
                           dmat_ref, qd_ref, kd_ref, cd_ref, sf_ref, sb_ref, sb_hist_ref, kvf_hist_ref,
                           kseq_ref, vseq_ref, useq_ref,
                           *, seq_len, n_tiles):
    c = RET_CHUNK
    tl = a2_ref.shape[1]
    chunks = tl // c
    j = pl.program_id(1)
    rot = (cos_ref[...], sin_ref[...])

    @pl.when((pl.program_id(0) == 0) & (j == 0))
    def _():
        _ret_tables(dec_f_ref, dec_b_ref, dmat_ref, qd_ref, kd_ref, cd_ref)
        zeros = jnp.zeros((POOL_HALO, RET_WIDTH), F32)
        useq_ref[0:POOL_HALO, :] = zeros
        useq_ref[POOL_HALO + seq_len:, :] = zeros

    @pl.when(j == 0)
    def _():
        for h in range(N_HEADS):
            sb_ref[h] = s0b_ref[0, h]
            sf_ref[h] = s0f_ref[0, h]

    @pl.when(j < n_tiles)
    def _():
        tile = n_tiles - 1 - j
        t0 = pl.multiple_of(tile * tl, tl)
        a2 = a2_ref[0]
        k = _project(a2, win_ref, 1, rot).astype(BF16)
        v = _project(a2, win_ref, 2, rot).astype(BF16)
        kseq_ref[pl.ds(t0, tl), :] = k
        vseq_ref[pl.ds(t0, tl), :] = v
        useq_ref[pl.ds(t0 + POOL_HALO, tl), :] = _project(a2, win_ref, 4, rot)
        for ci in reversed(range(chunks)):
            n = tile * chunks + ci
            rows = slice(ci * c, (ci + 1) * c)
            for h in range(N_HEADS):
                kv = _chunk_kv(k[rows, _head(h)], v[rows, _head(h)], kd_ref[h])
                s_b = sb_ref[h]
                sb_hist_ref[n, h] = s_b.astype(BF16)
                kvf_hist_ref[n, h] = kv[:, :HEAD_DIM]
                sb_ref[h] = s_b * cd_ref[h][:, HEAD_DIM:] + kv[:, HEAD_DIM:]

    @pl.when(j >= n_tiles)
    def _():
        tile = j - n_tiles
        t0 = pl.multiple_of(tile * tl, tl)
        a2 = a2_ref[0]
        q = _project(a2, win_ref, 0, rot).astype(BF16)
        gate = _project(a2, win_ref, 3, rot)
        for ci in range(chunks):
            n = tile * chunks + ci
            rows = slice(ci * c, (ci + 1) * c)
            seq_rows = pl.ds(t0 + ci * c, c)
            for h in range(N_HEADS):
                q_c = q[rows, _head(h)]
                s_f = sf_ref[h]
                s2 = jnp.concatenate([s_f.astype(BF16), sb_hist_ref[n, h]], axis=1)
                cross = _dot(q_c, s2) * qd_ref[h]
                mix_ref[0, rows, _head(h)] = _ret_out(q_c, kseq_ref[seq_rows, _head(h)], vseq_ref[seq_rows, _head(h)],
                                                      dmat_ref[h], cross, gate[rows, _head(h)], gn_ref[:, _head(h)])
                sf_ref[h] = s_f * cd_ref[h][:, :HEAD_DIM] + kvf_hist_ref[n, h]

        def store(g, val):
            mix_ref[0, :, _head(N_HEADS + g)] = val
        _pool_groups(lambda g: useq_ref[pl.ds(t0, tl + 2 * POOL_HALO), _head(g)], pw_ref, ps_ref,
                     tile * tl, tl, seq_len, store)


def _stage_b_latent(a2, rot, w_in, dec_f, dec_b, gn, pool_w, pool_scale, s0f, s0b):
    b, l, d = a2.shape
    tl = RET_TILE
    nt = l // tl
    n_chunks = l // RET_CHUNK
    both_tile = lambda j: jnp.where(j < nt, nt - 1 - j, j - nt)
    fwd = lambda: pl.BlockSpec((1, tl, MIX_WIDTH), lambda i, j: (i, jnp.maximum(j - nt, 0), 0))
    rot_spec = lambda: pl.BlockSpec((tl, HEAD_DIM), lambda i, j: (both_tile(j), 0))
    state = lambda: pl.BlockSpec((1, N_HEADS, HEAD_DIM, HEAD_DIM), lambda i, j: (i, 0, 0, 0))
    return pl.pallas_call(
        functools.partial(_stage_b_latent_kernel, seq_len=l, n_tiles=nt),
        out_shape=jax.ShapeDtypeStruct((b, l, MIX_WIDTH), BF16),
        grid=(b, 2 * nt),
        in_specs=[pl.BlockSpec((1, tl, d), lambda i, j: (i, both_tile(j), 0)), rot_spec(), rot_spec(),
                  _resident(w_in.shape), _resident(dec_f.shape), _resident(dec_b.shape), _resident(gn.shape),
                  _resident(pool_w.shape), _resident(pool_scale.shape), state(), state()],
        out_specs=fwd(),
        scratch_shapes=_ret_table_scratch() + [
            pltpu.VMEM((N_HEADS, HEAD_DIM, HEAD_DIM), F32), pltpu.VMEM((N_HEADS, HEAD_DIM, HEAD_DIM), F32),
            pltpu.VMEM((n_chunks, N_HEADS, HEAD_DIM, HEAD_DIM), BF16),
            pltpu.VMEM((n_chunks, N_HEADS, HEAD_DIM, HEAD_DIM), F32),
            pltpu.VMEM((l, RET_WIDTH), BF16), pltpu.VMEM((l, RET_WIDTH), BF16),
            pltpu.VMEM((l + 2 * POOL_HALO, RET_WIDTH), F32)],
        compiler_params=pltpu.CompilerParams(
            dimension_semantics=("arbitrary", "arbitrary"), vmem_limit_bytes=V7X_VMEM_LIMIT),
        name="stage_b_latent",
    )(a2, *rot, w_in, dec_f, dec_b, gn, pool_w, pool_scale, s0f, s0b)


def _rotary_tables(seq_len):
    rows = seq_len // GRID_W
    row = np.repeat(np.arange(rows, dtype=np.float64), GRID_W)
    col = np.tile(np.arange(GRID_W, dtype=np.float64), rows)
    n_half = HEAD_DIM // 4
    freqs = ROPE_BASE ** (-np.arange(n_half, dtype=np.float64) / n_half)
    ang = np.concatenate([row[:, None] * freqs, col[:, None] * freqs], axis=-1)
    cos, sin = np.cos(ang), np.sin(ang)
    cos2 = np.concatenate([cos, cos], axis=-1).astype(np.float32)
    sin2 = np.concatenate([-sin, sin], axis=-1).astype(np.float32)
    return jnp.asarray(cos2), jnp.asarray(sin2)


def kernel(x_prompt, x_sample, state_ret_fwd, state_ret_bwd, c, c_ctx, ada_w, ada_b, norm_ffn1, ffn1_w1, ffn1_w3, ffn1_w2, norm_mix, w_in, ret_decay_fwd, ret_decay_bwd, ret_gn, pool_w, pool_scale, w_out, norm_ffn2, ffn2_w1, ffn2_w3, ffn2_w2, norm_final):
    depth = ada_w.shape[0]
    assert depth == 1, "single trunk layer"
    d = D_MODEL
    n_ctx, l_ctx, _ = x_prompt.shape
    n_lat, l_lat, _ = x_sample.shape
    t_ctx, t_lat = n_ctx * l_ctx, n_lat * l_lat

    cond = jnp.concatenate([c_ctx[None, :], c, jnp.zeros((8 - 1 - n_lat, d), F32)], axis=0)
    mods = _mods(cond, ada_w[0], ada_b[0])

    row = lambda g: g.reshape(1, -1)
    w_in_b = w_in[0].astype(BF16)
    n1, nmix, n2, nf = row(norm_ffn1[0]), row(norm_mix[0]), row(norm_ffn2[0]), row(norm_final)
    dec_f, dec_b = ret_decay_fwd[0].reshape(1, N_HEADS), ret_decay_bwd[0].reshape(1, N_HEADS)
    gn, ps = row(ret_gn[0]), row(pool_scale[0])

    h1c, a2c, ffn1_b = _stage_a(x_prompt.reshape(t_ctx, d), mods, 0, t_ctx, n1, ffn1_w1[0], ffn1_w3[0], ffn1_w2[0],
                                nmix, "stage_a_context")
    mixc, new_f, new_b = _stage_b_context(a2c, w_in_b, dec_f, dec_b, gn, pool_w[0], ps, n_ctx, l_ctx)
    y_prompt, (w_out_b, *ffn2_b) = _stage_c(h1c, mixc, mods, 0, t_ctx, w_out[0], n2,
                                           ffn2_w1[0], ffn2_w3[0], ffn2_w2[0], nf, "stage_c_context")

    h1l, a2l, _ = _stage_a(x_sample.reshape(t_lat, d), mods, 1, l_lat, n1, *ffn1_b, nmix, "stage_a_latent")
    mixl = _stage_b_latent(a2l.reshape(n_lat, l_lat, d), _rotary_tables(l_lat), w_in_b, dec_f, dec_b, gn,
                           pool_w[0], ps, state_ret_fwd[:, 0], state_ret_bwd[:, 0])
    y_sample, _ = _stage_c(h1l, mixl.reshape(t_lat, MIX_WIDTH), mods, 1, l_lat,
                           w_out_b, n2, *ffn2_b, nf, "stage_c_latent")

    return (y_prompt.reshape(n_ctx, l_ctx, d), y_sample.reshape(n_lat, l_lat, d), new_f, new_b)
```

```python
import functools

import jax
import jax.numpy as jnp
import numpy as np
from jax import lax
from jax.experimental import pallas as pl
from jax.experimental.pallas import tpu as pltpu

D_MODEL = 1024
GRID_W = 64
N_HEADS = 4
HEAD_DIM = 128
RET_WIDTH = N_HEADS * HEAD_DIM
MIX_WIDTH = 2 * RET_WIDTH
POOL_WINDOWS = (2, 4, 8, 16)
POOL_HALO = 8
D_FF = 2816
ROPE_BASE = 10000.0
N_MOD = 9
EPS = 1e-6

FF_COLS = 256
SUB_TILE = 512
FFN_SUB_TILES = 2
TOKEN_TILE = SUB_TILE * FFN_SUB_TILES
RET_TILE = 1024
RET_CHUNK = 256
COL_STAGE_DEPTH = 4
ROW_STAGE_DEPTH = 2
PANEL_SPLIT = 4
V7X_VMEM_LIMIT = 56 * 1024 * 1024

F32 = jnp.float32
BF16 = jnp.bfloat16


def _silu(x):
    return x * (1.0 / (1.0 + jnp.exp(-x)))


def _rmsnorm(x, g):
    return x * lax.rsqrt(jnp.mean(x * x, axis=-1, keepdims=True) + EPS) * g


def _dot(a, b):
    return jnp.dot(a, b, preferred_element_type=F32)


def _dot_tn(a, b):
    return lax.dot_general(a, b, (((0,), (0,)), ((), ())), preferred_element_type=F32)


def _dot_nt(a, b):
    return lax.dot_general(a, b, (((1,), (1,)), ((), ())), preferred_element_type=F32)


def _head(h):
    return slice(h * HEAD_DIM, (h + 1) * HEAD_DIM)


def _resident(shape):
    nd = len(shape)
    return pl.BlockSpec(shape, lambda *_: (0,) * nd, pipeline_mode=pl.Buffered(1))


def _hbm():
    return pl.BlockSpec(memory_space=pl.ANY)


def _mods_kernel(cond_ref, w_ref, b_ref, o_ref):
    s = _silu(cond_ref[...]).astype(BF16)
    o_ref[...] = _dot(s, w_ref[...].astype(BF16)) + b_ref[...]


def _mods(cond, ada_w, ada_b):
    n = N_MOD * D_MODEL
    out = pl.pallas_call(
        _mods_kernel,
        out_shape=jax.ShapeDtypeStruct((8, n), F32),
        grid=(N_MOD,),
        in_specs=[
            pl.BlockSpec((8, D_MODEL), lambda j: (0, 0)),
            pl.BlockSpec((D_MODEL, D_MODEL), lambda j: (0, j)),
            pl.BlockSpec((1, D_MODEL), lambda j: (0, j)),
        ],
        out_specs=pl.BlockSpec((8, D_MODEL), lambda j: (0, j)),
        name="adaln_mods",
    )(cond, ada_w, ada_b.reshape(1, n))
    return out.reshape(8, N_MOD, D_MODEL)


def _swiglu(a_bf16, w1_ref, w3_ref, w2_ref, act_ref):
    for c in range(D_FF // FF_COLS):
        cols = slice(c * FF_COLS, (c + 1) * FF_COLS)
        g = _dot(a_bf16, w1_ref[:, cols])
        u = _dot(a_bf16, w3_ref[:, cols])
        act_ref[:, cols] = (_silu(g) * u).astype(BF16)
    return _dot(act_ref[...], w2_ref[...])


def _panel_stream(srcs, dsts, stage_ref, sem_ref):
    depth, rows = stage_ref.shape[0], stage_ref.shape[1]
    part = rows // PANEL_SPLIT

    def copies(k):
        slot = k % depth
        return [pltpu.make_async_copy(srcs[k].at[pl.ds(s * part, part), :],
                                      stage_ref.at[slot, pl.ds(s * part, part), :], sem_ref.at[slot])
                for s in range(PANEL_SPLIT)]

    def start(k):
        for cp in copies(k):
            cp.start()

    def prime():
        for k in range(min(depth, len(srcs))):
            start(k)

    def take(k):
        for cp in copies(k):
            cp.wait()
        dsts[k][...] = stage_ref[k % depth].astype(BF16)
        if k + depth < len(srcs):
            start(k + depth)

    return prime, take


def _col_panels(ref):
    return [ref.at[:, pl.ds(p * FF_COLS, FF_COLS)] for p in range(ref.shape[1] // FF_COLS)]


def _row_panels(ref):
    return [ref.at[pl.ds(p * FF_COLS, FF_COLS), :] for p in range(ref.shape[0] // FF_COLS)]


def _load_weights(col_pairs, row_pairs, col_stage, col_sem, row_stage, row_sem):
    col_src = [p for src, _ in col_pairs for p in _col_panels(src)]
    col_dst = [p for _, dst in col_pairs for p in _col_panels(dst)]
    row_src = [p for src, _ in row_pairs for p in _row_panels(src)]
    row_dst = [p for _, dst in row_pairs for p in _row_panels(dst)]
    col_prime, col_take = _panel_stream(col_src, col_dst, col_stage, col_sem)
    row_prime, row_take = _panel_stream(row_src, row_dst, row_stage, row_sem)
    col_prime()
    row_prime()
    n_col, n_row = len(col_src), len(row_src)
    for k in range(max(n_col, n_row)):
        for kc in range(k * n_col // max(n_col, n_row), (k + 1) * n_col // max(n_col, n_row)):
            col_take(kc)
        for kr in range(k * n_row // max(n_col, n_row), (k + 1) * n_row // max(n_col, n_row)):
            row_take(kr)


def _export_copies(vmem_refs, hbm_refs, sem_ref):
    return [pltpu.make_async_copy(src, dst, sem_ref.at[k]) for k, (src, dst) in enumerate(zip(vmem_refs, hbm_refs))]


def _converting_call(compute, hbm_weights, col_major, vmem_weights, exports, stage_refs, export_sem):
    step = pl.program_id(0)
    pairs = list(zip(hbm_weights, vmem_weights))

    @pl.when(step == 0)
    def _():
        _load_weights([p for p, c in zip(pairs, col_major) if c], [p for p, c in zip(pairs, col_major) if not c],
                      *stage_refs)
        for cp in _export_copies(vmem_weights, exports, export_sem):
            cp.start()

    @pl.when(step > 0)
    def _():
        compute()

    @pl.when(step == pl.num_programs(0) - 1)
    def _():
        for cp in _export_copies(vmem_weights, exports, export_sem):
            cp.wait()


def _stage_a_compute(x_ref, mods_ref, n1_ref, nmix_ref, h1_ref, a2_ref, act_ref, w1_ref, w3_ref, w2_ref):
    mods = mods_ref[0]
    sh1, sc1, g1, sh2, sc2 = mods[0:1], mods[1:2], mods[2:3], mods[3:4], mods[4:5]
    for s in range(FFN_SUB_TILES):
        rows = slice(s * SUB_TILE, (s + 1) * SUB_TILE)
        x = x_ref[rows, :]
        a1 = _rmsnorm(x, n1_ref[...]) * (1.0 + sc1) + sh1
        h1 = x + (0.5 * g1) * _swiglu(a1.astype(BF16), w1_ref, w3_ref, w2_ref, act_ref.at[s])
        h1_ref[rows, :] = h1
        a2_ref[rows, :] = (_rmsnorm(h1, nmix_ref[...]) * (1.0 + sc2) + sh2).astype(BF16)


def _stage_a_converting_kernel(x_ref, mods_ref, n1_ref, w1_hbm, w3_hbm, w2_hbm, nmix_ref,
                               h1_ref, a2_ref, w1_out, w3_out, w2_out,
                               act_ref, w1_ref, w3_ref, w2_ref, col_stage, col_sem, row_stage, row_sem, export_sem):
    compute = functools.partial(_stage_a_compute, x_ref, mods_ref, n1_ref, nmix_ref, h1_ref, a2_ref, act_ref,
                                w1_ref, w3_ref, w2_ref)
    _converting_call(compute, (w1_hbm, w3_hbm, w2_hbm), (True, True, False), (w1_ref, w3_ref, w2_ref),
                     (w1_out, w3_out, w2_out), (col_stage, col_sem, row_stage, row_sem), export_sem)


def _stage_a_kernel(x_ref, mods_ref, n1_ref, w1_ref, w3_ref, w2_ref, nmix_ref, h1_ref, a2_ref, act_ref):
    _stage_a_compute(x_ref, mods_ref, n1_ref, nmix_ref, h1_ref, a2_ref, act_ref, w1_ref, w3_ref, w2_ref)


def _stage_c_compute(h1_ref, mixin_ref, mods_ref, n2_ref, nf_ref, y_ref, act_ref,
                     wout_ref, w1_ref, w3_ref, w2_ref):
    mods = mods_ref[0]
    g2, sh3, sc3, g3 = mods[5:6], mods[6:7], mods[7:8], mods[8:9]
    for s in range(FFN_SUB_TILES):
        rows = slice(s * SUB_TILE, (s + 1) * SUB_TILE)
        h2 = h1_ref[rows, :] + g2 * _dot(mixin_ref[rows, :], wout_ref[...])
        a3 = _rmsnorm(h2, n2_ref[...]) * (1.0 + sc3) + sh3
        h3 = h2 + (0.5 * g3) * _swiglu(a3.astype(BF16), w1_ref, w3_ref, w2_ref, act_ref.at[s])
        y_ref[rows, :] = _rmsnorm(h3, nf_ref[...])


def _stage_c_converting_kernel(h1_ref, mixin_ref, mods_ref, wout_hbm, n2_ref, w1_hbm, w3_hbm, w2_hbm, nf_ref,
                               y_ref, wout_out, w1_out, w3_out, w2_out,
                               act_ref, w1_ref, w3_ref, w2_ref, col_stage, col_sem, row_stage, row_sem, export_sem,
                               wout_ref):
    compute = functools.partial(_stage_c_compute, h1_ref, mixin_ref, mods_ref, n2_ref, nf_ref, y_ref, act_ref,
                                wout_ref, w1_ref, w3_ref, w2_ref)
    _converting_call(compute, (wout_hbm, w1_hbm, w3_hbm, w2_hbm), (True, True, True, False),
                     (wout_ref, w1_ref, w3_ref, w2_ref), (wout_out, w1_out, w3_out, w2_out),
                     (col_stage, col_sem, row_stage, row_sem), export_sem)


def _stage_c_kernel(h1_ref, mixin_ref, mods_ref, wout_ref, n2_ref, w1_ref, w3_ref, w2_ref, nf_ref,
                    y_ref, act_ref):
    _stage_c_compute(h1_ref, mixin_ref, mods_ref, n2_ref, nf_ref, y_ref, act_ref,
                     wout_ref, w1_ref, w3_ref, w2_ref)


def _ffn_stage_specs(weight, mod_row0, rows_per_mod):
    converting = weight.dtype == F32
    lead = 1 if converting else 0
    tile_of = lambda i: jnp.maximum(i - lead, 0)
    tile = lambda width: pl.BlockSpec((TOKEN_TILE, width), lambda i: (tile_of(i), 0))
    tiles_per_mod = rows_per_mod // TOKEN_TILE
    mods_spec = pl.BlockSpec((1, N_MOD, D_MODEL), lambda i: (mod_row0 + tile_of(i) // tiles_per_mod, 0, 0))
    weight_spec = (lambda w: _hbm()) if converting else (lambda w: _resident(w.shape))
    return tile, mods_spec, weight_spec, converting, lead


def _act_scratch():
    return [pltpu.VMEM((FFN_SUB_TILES, SUB_TILE, D_FF), BF16)]


def _conversion_scratch(weights):
    return ([pltpu.VMEM(w.shape, BF16) for w in weights]
            + [pltpu.VMEM((COL_STAGE_DEPTH, D_MODEL, FF_COLS), F32), pltpu.SemaphoreType.DMA((COL_STAGE_DEPTH,)),
               pltpu.VMEM((ROW_STAGE_DEPTH, FF_COLS, D_MODEL), F32), pltpu.SemaphoreType.DMA((ROW_STAGE_DEPTH,)),
               pltpu.SemaphoreType.DMA((len(weights),))])


def _stage_a(x, mods, mod_row0, rows_per_mod, n1, w1, w3, w2, nmix, name):
    tile, mods_spec, weight_spec, converting, lead = _ffn_stage_specs(w1, mod_row0, rows_per_mod)
    tokens, d, tm = x.shape[0], D_MODEL, TOKEN_TILE
    out_shape = [jax.ShapeDtypeStruct((tokens, d), F32), jax.ShapeDtypeStruct((tokens, d), BF16)]
    out_specs = [tile(d), tile(d)]
    scratch = _act_scratch()
    if converting:
        out_shape += [jax.ShapeDtypeStruct(w.shape, BF16) for w in (w1, w3, w2)]
        out_specs += [_hbm()] * 3
        scratch += _conversion_scratch((w1, w3, w2))
    outs = pl.pallas_call(
        _stage_a_converting_kernel if converting else _stage_a_kernel,
        out_shape=out_shape,
        grid=(lead + tokens // tm,),
        in_specs=[tile(d), mods_spec, _resident((1, d)), weight_spec(w1), weight_spec(w3), weight_spec(w2),
                  _resident((1, d))],
        out_specs=out_specs,
        scratch_shapes=scratch,
        compiler_params=pltpu.CompilerParams(
            dimension_semantics=("arbitrary",), vmem_limit_bytes=V7X_VMEM_LIMIT),
        name=name,
    )(x, mods, n1, w1, w3, w2, nmix)
    return outs[0], outs[1], tuple(outs[2:])


def _stage_c(h1, mixin, mods, mod_row0, rows_per_mod, w_out, n2, w1, w3, w2, nf, name):
    tile, mods_spec, weight_spec, converting, lead = _ffn_stage_specs(w1, mod_row0, rows_per_mod)
    tokens, d, tm = h1.shape[0], D_MODEL, TOKEN_TILE
    out_shape = [jax.ShapeDtypeStruct((tokens, d), F32)]
    out_specs = [tile(d)]
    scratch = _act_scratch()
    if converting:
        out_shape += [jax.ShapeDtypeStruct(w.shape, BF16) for w in (w_out, w1, w3, w2)]
        out_specs += [_hbm()] * 4
        conv = _conversion_scratch((w1, w3, w2))
        conv[-1] = pltpu.SemaphoreType.DMA((4,))
        scratch += conv + [pltpu.VMEM(w_out.shape, BF16)]
    outs = pl.pallas_call(
        _stage_c_converting_kernel if converting else _stage_c_kernel,
        out_shape=out_shape,
        grid=(lead + tokens // tm,),
        in_specs=[tile(d), tile(MIX_WIDTH), mods_spec,
                  weight_spec(w_out), _resident((1, d)), weight_spec(w1), weight_spec(w3), weight_spec(w2),
                  _resident((1, d))],
        out_specs=out_specs,
        scratch_shapes=scratch,
        compiler_params=pltpu.CompilerParams(
            dimension_semantics=("arbitrary",), vmem_limit_bytes=V7X_VMEM_LIMIT),
        name=name,
    )(h1, mixin, mods, w_out, n2, w1, w3, w2, nf)
    return outs[0], tuple(outs[1:])


def _ret_tables(dec_f_ref, dec_b_ref, dmat_ref, qd_ref, kd_ref, cd_ref):
    c = RET_CHUNK
    row = lax.broadcasted_iota(jnp.int32, (c, c), 0)
    col = lax.broadcasted_iota(jnp.int32, (c, c), 1)
    rel = (row - col).astype(F32)
    r = row[:, :HEAD_DIM].astype(F32)
    for h in range(N_HEADS):
        lg_f = -jnp.exp(jnp.broadcast_to(dec_f_ref[0:1, h:h + 1], (1, c)))
        lg_b = -jnp.exp(jnp.broadcast_to(dec_b_ref[0:1, h:h + 1], (1, c)))
        dmat_ref[h] = (jnp.where(rel >= 0, jnp.exp(lg_f * jnp.maximum(rel, 0.0)), 0.0)
                       + jnp.where(rel <= 0, jnp.exp(lg_b * jnp.maximum(-rel, 0.0)), 0.0))
        lf, lb = lg_f[:, :HEAD_DIM], lg_b[:, :HEAD_DIM]
        qd_ref[h] = jnp.concatenate([jnp.exp(lf * (r + 1.0)), jnp.exp(lb * (c - r))], axis=1)
        kd_ref[h] = jnp.concatenate([jnp.exp(lf * (c - 1.0 - r)), jnp.exp(lb * r)], axis=1)
        cd_ref[h] = jnp.concatenate([jnp.exp(lf * c), jnp.exp(lb * c)], axis=1)


def _ret_table_scratch():
    c = RET_CHUNK
    return [pltpu.VMEM((N_HEADS, c, c), F32), pltpu.VMEM((N_HEADS, c, 2 * HEAD_DIM), F32),
            pltpu.VMEM((N_HEADS, c, 2 * HEAD_DIM), F32), pltpu.VMEM((N_HEADS, 1, 2 * HEAD_DIM), F32)]


def _project(a2, win_ref, part, rot):
    t = _dot(a2, win_ref[:, part * RET_WIDTH:(part + 1) * RET_WIDTH])
    if part == 1:
        t = t * (HEAD_DIM ** -0.5)
    if part < 2 and rot is not None:
        cos2, sin2 = rot
        t = jnp.concatenate([t[:, _head(h)] * cos2 + pltpu.roll(t[:, _head(h)], HEAD_DIM // 2, axis=1) * sin2
                             for h in range(N_HEADS)], axis=-1)
    return t


def _chunk_kv(k_c, v_c, kd):
    v32 = v_c.astype(F32)
    vd = (jnp.concatenate([v32, v32], axis=1) * kd).astype(BF16)
    return _dot_tn(k_c, vd)


def _pool_inv_count(t0, rows, seq_len, window):
    def edge(start):
        t = start + lax.broadcasted_iota(jnp.int32, (POOL_HALO, HEAD_DIM), 0)
        cnt = jnp.minimum(t + window // 2, seq_len) - jnp.maximum(t - window // 2, 0)
        return 1.0 / cnt.astype(F32)
    inner = jnp.full((rows - 2 * POOL_HALO, HEAD_DIM), 1.0 / window, F32)
    return jnp.concatenate([edge(t0), inner, edge(t0 + rows - POOL_HALO)], axis=0)


def _pool_centred(ext, t0, rows, seq_len, window):
    n_ext = rows + 2 * POOL_HALO
    half = window // 2
    acc, span = ext, 1
    while span < half:
        acc = acc + pltpu.roll(acc, n_ext - span, axis=0)
        span *= 2
    acc = acc + pltpu.roll(acc, half, axis=0)
    win = acc[POOL_HALO:POOL_HALO + rows]
    tok = ext[POOL_HALO:POOL_HALO + rows]
    return win * _pool_inv_count(t0, rows, seq_len, window) - tok


def _pool_groups(ext_of_group, pw_ref, ps_ref, t0, rows, seq_len, store):
    for g, window in enumerate(POOL_WINDOWS):
        centred = _pool_centred(ext_of_group(g), t0, rows, seq_len, window)
        store(g, (_dot(centred.astype(BF16), pw_ref[g].astype(BF16)) * ps_ref[:, _head(g)]).astype(BF16))


def _stage_b_context_kernel(a2_ref, win_ref, dec_f_ref, dec_b_ref, gn_ref, pw_ref, ps_ref,
                            mix_ref, sf_ref, sb_ref, dmat_ref, qd_ref, kd_ref, cd_ref, ext_ref,
                            *, seq_len):
    @pl.when(pl.program_id(0) == 0)
    def _():
        _ret_tables(dec_f_ref, dec_b_ref, dmat_ref, qd_ref, kd_ref, cd_ref)
        zeros = jnp.zeros((POOL_HALO, RET_WIDTH), F32)
        for s in range(ext_ref.shape[0]):
            ext_ref[s, 0:POOL_HALO, :] = zeros
            ext_ref[s, POOL_HALO + seq_len:, :] = zeros

    half = a2_ref.shape[0] // 2
    seqs_per_half = half // seq_len

    def projection(r0):
        a2 = a2_ref[r0:r0 + half, :]
        out = {}

        def part(p):
            def run():
                t = _project(a2, win_ref, p, None)
                out[p] = t.astype(BF16) if p < 3 else t
            return run
        return out, [part(p) for p in range(5)]

    def mixing(out, r0):
        units = [(s, h, slice(s * seq_len, (s + 1) * seq_len)) for s in range(seqs_per_half) for h in range(N_HEADS)]
        tile_rows = lambda rows: slice(r0 + rows.start, r0 + rows.stop)
        st = {}

        def scores():
            st["p"] = [(_dot_nt(out[0][rows, _head(h)], out[1][rows, _head(h)]) * dmat_ref[h]).astype(BF16)
                       for s, h, rows in units]

        def values():
            st["o"] = [_dot(p, out[2][rows, _head(h)]) for p, (s, h, rows) in zip(st["p"], units)]

        def outputs():
            for o, (s, h, rows) in zip(st["o"], units):
                o = o * lax.rsqrt(jnp.mean(o * o, axis=-1, keepdims=True) + EPS) * gn_ref[:, _head(h)]
                mix_ref[tile_rows(rows), _head(h)] = (o * _silu(out[3][rows, _head(h)])).astype(BF16)

        def states():
            for s, h, rows in units:
                kv = _chunk_kv(out[1][rows, _head(h)], out[2][rows, _head(h)], kd_ref[h])
                seq = r0 // seq_len + s
                sf_ref[seq, 0, h] = kv[:, :HEAD_DIM]
                sb_ref[seq, 0, h] = kv[:, HEAD_DIM:]

        def pooling():
            for s in range(seqs_per_half):
                rows = slice(s * seq_len, (s + 1) * seq_len)
                seq = r0 // seq_len + s
                ext_ref[seq, POOL_HALO:POOL_HALO + seq_len, :] = out[4][rows, :]

                def store(g, val, rows=rows):
                    mix_ref[tile_rows(rows), _head(N_HEADS + g)] = val
                _pool_groups(lambda g, seq=seq: ext_ref[seq, :, _head(g)], pw_ref, ps_ref, 0, seq_len, seq_len, store)
        return [scores, values, outputs, states, pooling]

    out_a, parts_a = projection(0)
    for run in parts_a:
        run()
    out_b, parts_b = projection(half)
    for run_mxu, run_vpu in zip(parts_b, mixing(out_a, 0)):
        run_mxu()
        run_vpu()
    for run in mixing(out_b, half):
        run()


def _stage_b_context(a2, w_in, dec_f, dec_b, gn, pool_w, pool_scale, n_seq, seq_len):
    assert seq_len == RET_CHUNK
    tokens = n_seq * seq_len
    tl = RET_TILE
    seqs = tl // seq_len
    tile = lambda width: pl.BlockSpec((tl, width), lambda i: (i, 0))
    state_shape = jax.ShapeDtypeStruct((n_seq, 1, N_HEADS, HEAD_DIM, HEAD_DIM), F32)
    state_spec = lambda: pl.BlockSpec((seqs, 1, N_HEADS, HEAD_DIM, HEAD_DIM), lambda i: (i, 0, 0, 0, 0))
    return pl.pallas_call(
        functools.partial(_stage_b_context_kernel, seq_len=seq_len),
        out_shape=[jax.ShapeDtypeStruct((tokens, MIX_WIDTH), BF16)] + [state_shape] * 2,
        grid=(tokens // tl,),
        in_specs=[tile(D_MODEL), _resident(w_in.shape),
                  _resident(dec_f.shape), _resident(dec_b.shape), _resident(gn.shape),
                  _resident(pool_w.shape), _resident(pool_scale.shape)],
        out_specs=[tile(MIX_WIDTH), state_spec(), state_spec()],
        scratch_shapes=_ret_table_scratch() + [pltpu.VMEM((seqs, seq_len + 2 * POOL_HALO, RET_WIDTH), F32)],
        compiler_params=pltpu.CompilerParams(
            dimension_semantics=("arbitrary",), vmem_limit_bytes=V7X_VMEM_LIMIT),
        name="stage_b_context",
    )(a2, w_in, dec_f, dec_b, gn, pool_w, pool_scale)


def _stage_b_latent_kernel(a2_ref, cos_ref, sin_ref, win_ref, dec_f_ref, dec_b_ref, gn_ref, pw_ref, ps_ref,
                           s0f_ref, s0b_ref,
                           mix_ref,
                           dmat_ref, qd_ref, kd_ref, cd_ref, sf_ref, sb_ref, sb_hist_ref, kvf_hist_ref,
                           kseq_ref, vseq_ref, useq_ref, sf_tile_ref,
                           *, seq_len, n_tiles):
    c = RET_CHUNK
    tl = a2_ref.shape[1]
    chunks = tl // c
    j = pl.program_id(1)
    rot = (cos_ref[...], sin_ref[...])

    @pl.when((pl.program_id(0) == 0) & (j == 0))
    def _():
        _ret_tables(dec_f_ref, dec_b_ref, dmat_ref, qd_ref, kd_ref, cd_ref)
        zeros = jnp.zeros((POOL_HALO, RET_WIDTH), F32)
        useq_ref[0:POOL_HALO, :] = zeros
        useq_ref[POOL_HALO + seq_len:, :] = zeros

    @pl.when(j == 0)
    def _():
        for h in range(N_HEADS):
            sb_ref[h] = s0b_ref[0, h]
            sf_ref[h] = s0f_ref[0, h]

    @pl.when(j < n_tiles)
    def _():
        tile = n_tiles - 1 - j
        t0 = pl.multiple_of(tile * tl, tl)
        a2 = a2_ref[0]
        k = _project(a2, win_ref, 1, rot).astype(BF16)
        v = _project(a2, win_ref, 2, rot).astype(BF16)
        kseq_ref[pl.ds(t0, tl), :] = k
        vseq_ref[pl.ds(t0, tl), :] = v
        useq_ref[pl.ds(t0 + POOL_HALO, tl), :] = _project(a2, win_ref, 4, rot)
        for ci in reversed(range(chunks)):
            n = tile * chunks + ci
            rows = slice(ci * c, (ci + 1) * c)
            for h in range(N_HEADS):
                kv = _chunk_kv(k[rows, _head(h)], v[rows, _head(h)], kd_ref[h])
                s_b = sb_ref[h]
                sb_hist_ref[n, h] = s_b.astype(BF16)
                kvf_hist_ref[n, h] = kv[:, :HEAD_DIM]
                sb_ref[h] = s_b * cd_ref[h][:, HEAD_DIM:] + kv[:, HEAD_DIM:]

    @pl.when(j >= n_tiles)
    def _():
        tile = j - n_tiles
        t0 = pl.multiple_of(tile * tl, tl)
        half = tl // 2

        for h in range(N_HEADS):
            s_f = sf_ref[h]
            for ci in range(chunks):
                sf_tile_ref[ci, h] = s_f.astype(BF16)
                s_f = s_f * cd_ref[h][:, :HEAD_DIM] + kvf_hist_ref[tile * chunks + ci, h]
            sf_ref[h] = s_f

        def projection(r0):
            a2 = a2_ref[0, r0:r0 + half, :]
            rot_rows = (cos_ref[r0:r0 + half, :], sin_ref[r0:r0 + half, :])
            out = {}

            def part(p):
                def run():
                    t = _project(a2, win_ref, p, rot_rows)
                    out[p] = t.astype(BF16) if p == 0 else t
                return run
            return out, [part(0), part(3)]

        def mixing(out, r0):
            units = [(r0 // c + ci, h, slice(ci * c, (ci + 1) * c)) for ci in range(half // c) for h in range(N_HEADS)]
            seq_rows = lambda rows: pl.ds(t0 + r0 + rows.start, c)
            st = {}

            def scores():
                st["p"] = [(_dot_nt(out[0][rows, _head(h)], kseq_ref[seq_rows(rows), _head(h)])
                            * dmat_ref[h]).astype(BF16) for ct, h, rows in units]

            def values():
                st["o"] = []
                for p, (ct, h, rows) in zip(st["p"], units):
                    s2 = jnp.concatenate([sf_tile_ref[ct, h], sb_hist_ref[tile * chunks + ct, h]], axis=1)
                    cross = _dot(out[0][rows, _head(h)], s2) * qd_ref[h]
                    st["o"].append(_dot(p, vseq_ref[seq_rows(rows), _head(h)])
                                   + cross[:, :HEAD_DIM] + cross[:, HEAD_DIM:])

            def outputs():
                for o, (ct, h, rows) in zip(st["o"], units):
                    o = o * lax.rsqrt(jnp.mean(o * o, axis=-1, keepdims=True) + EPS) * gn_ref[:, _head(h)]
                    mix_ref[0, r0 + rows.start:r0 + rows.stop, _head(h)] = (
                        o * _silu(out[3][rows, _head(h)])).astype(BF16)

            def pooling():
                def store(g, val):
                    mix_ref[0, r0:r0 + half, _head(N_HEADS + g)] = val
                _pool_groups(lambda g: useq_ref[pl.ds(t0 + r0, half + 2 * POOL_HALO), _head(g)], pw_ref, ps_ref,
                             tile * tl + r0, half, seq_len, store)
            return [scores, values, outputs, pooling]

        out_a, parts_a = projection(0)
        for run in parts_a:
            run()
        out_b, parts_b = projection(half)
        stages_a = mixing(out_a, 0)
        per_part = len(stages_a) // len(parts_b)
        for i, run_mxu in enumerate(parts_b):
            run_mxu()
            for run_vpu in stages_a[i * per_part:(i + 1) * per_part]:
                run_vpu()
        for run in mixing(out_b, half):
            run()


def _stage_b_latent(a2, rot, w_in, dec_f, dec_b, gn, pool_w, pool_scale, s0f, s0b):
    b, l, d = a2.shape
    tl = RET_TILE
    nt = l // tl
    n_chunks = l // RET_CHUNK
    both_tile = lambda j: jnp.where(j < nt, nt - 1 - j, j - nt)
    fwd = lambda: pl.BlockSpec((1, tl, MIX_WIDTH), lambda i, j: (i, jnp.maximum(j - nt, 0), 0))
    rot_spec = lambda: pl.BlockSpec((tl, HEAD_DIM), lambda i, j: (both_tile(j), 0))
    state = lambda: pl.BlockSpec((1, N_HEADS, HEAD_DIM, HEAD_DIM), lambda i, j: (i, 0, 0, 0))
    return pl.pallas_call(
        functools.partial(_stage_b_latent_kernel, seq_len=l, n_tiles=nt),
        out_shape=jax.ShapeDtypeStruct((b, l, MIX_WIDTH), BF16),
        grid=(b, 2 * nt),
        in_specs=[pl.BlockSpec((1, tl, d), lambda i, j: (i, both_tile(j), 0)), rot_spec(), rot_spec(),
                  _resident(w_in.shape), _resident(dec_f.shape), _resident(dec_b.shape), _resident(gn.shape),
                  _resident(pool_w.shape), _resident(pool_scale.shape), state(), state()],
        out_specs=fwd(),
        scratch_shapes=_ret_table_scratch() + [
            pltpu.VMEM((N_HEADS, HEAD_DIM, HEAD_DIM), F32), pltpu.VMEM((N_HEADS, HEAD_DIM, HEAD_DIM), F32),
            pltpu.VMEM((n_chunks, N_HEADS, HEAD_DIM, HEAD_DIM), BF16),
            pltpu.VMEM((n_chunks, N_HEADS, HEAD_DIM, HEAD_DIM), F32),
            pltpu.VMEM((l, RET_WIDTH), BF16), pltpu.VMEM((l, RET_WIDTH), BF16),
            pltpu.VMEM((l + 2 * POOL_HALO, RET_WIDTH), F32),
            pltpu.VMEM((tl // RET_CHUNK, N_HEADS, HEAD_DIM, HEAD_DIM), BF16)],
        compiler_params=pltpu.CompilerParams(
            dimension_semantics=("arbitrary", "arbitrary"), vmem_limit_bytes=V7X_VMEM_LIMIT),
        name="stage_b_latent",
    )(a2, *rot, w_in, dec_f, dec_b, gn, pool_w, pool_scale, s0f, s0b)


def _rotary_tables(seq_len):
    rows = seq_len // GRID_W
    row = np.repeat(np.arange(rows, dtype=np.float64), GRID_W)
    col = np.tile(np.arange(GRID_W, dtype=np.float64), rows)
    n_half = HEAD_DIM // 4
    freqs = ROPE_BASE ** (-np.arange(n_half, dtype=np.float64) / n_half)
    ang = np.concatenate([row[:, None] * freqs, col[:, None] * freqs], axis=-1)
    cos, sin = np.cos(ang), np.sin(ang)
    cos2 = np.concatenate([cos, cos], axis=-1).astype(np.float32)
    sin2 = np.concatenate([-sin, sin], axis=-1).astype(np.float32)
    return jnp.asarray(cos2), jnp.asarray(sin2)


def kernel(x_prompt, x_sample, state_ret_fwd, state_ret_bwd, c, c_ctx, ada_w, ada_b, norm_ffn1, ffn1_w1, ffn1_w3, ffn1_w2, norm_mix, w_in, ret_decay_fwd, ret_decay_bwd, ret_gn, pool_w, pool_scale, w_out, norm_ffn2, ffn2_w1, ffn2_w3, ffn2_w2, norm_final):
    depth = ada_w.shape[0]
    assert depth == 1, "single trunk layer"
    d = D_MODEL
    n_ctx, l_ctx, _ = x_prompt.shape
    n_lat, l_lat, _ = x_sample.shape
    t_ctx, t_lat = n_ctx * l_ctx, n_lat * l_lat

    cond = jnp.concatenate([c_ctx[None, :], c, jnp.zeros((8 - 1 - n_lat, d), F32)], axis=0)
    mods = _mods(cond, ada_w[0], ada_b[0])

    row = lambda g: g.reshape(1, -1)
    w_in_b = w_in[0].astype(BF16)
    n1, nmix, n2, nf = row(norm_ffn1[0]), row(norm_mix[0]), row(norm_ffn2[0]), row(norm_final)
    dec_f, dec_b = ret_decay_fwd[0].reshape(1, N_HEADS), ret_decay_bwd[0].reshape(1, N_HEADS)
    gn, ps = row(ret_gn[0]), row(pool_scale[0])

    h1c, a2c, ffn1_b = _stage_a(x_prompt.reshape(t_ctx, d), mods, 0, t_ctx, n1, ffn1_w1[0], ffn1_w3[0], ffn1_w2[0],
                                nmix, "stage_a_context")
    mixc, new_f, new_b = _stage_b_context(a2c, w_in_b, dec_f, dec_b, gn, pool_w[0], ps, n_ctx, l_ctx)
    y_prompt, (w_out_b, *ffn2_b) = _stage_c(h1c, mixc, mods, 0, t_ctx, w_out[0], n2,
                                           ffn2_w1[0], ffn2_w3[0], ffn2_w2[0], nf, "stage_c_context")

    h1l, a2l, _ = _stage_a(x_sample.reshape(t_lat, d), mods, 1, l_lat, n1, *ffn1_b, nmix, "stage_a_latent")
    mixl = _stage_b_latent(a2l.reshape(n_lat, l_lat, d), _rotary_tables(l_lat), w_in_b, dec_f, dec_b, gn,
                           pool_w[0], ps, state_ret_fwd[:, 0], state_ret_bwd[:, 0])
    y_sample, _ = _stage_c(h1l, mixl.reshape(t_lat, MIX_WIDTH), mods, 1, l_lat,
                           w_out_b, n2, *ffn2_b, nf, "stage_c_latent")

    return (y_prompt.reshape(n_ctx, l_ctx, d), y_sample.reshape(n_lat, l_lat, d), new_f, new_b)
```

```python
import functools

import jax
import jax.numpy as jnp
import numpy as np
from jax import lax
from jax.experimental import pallas as pl
from jax.experimental.pallas import tpu as pltpu

D_MODEL = 1024
GRID_W = 64
N_HEADS = 4
HEAD_DIM = 128
RET_WIDTH = N_HEADS * HEAD_DIM
MIX_WIDTH = 2 * RET_WIDTH
POOL_WINDOWS = (2, 4, 8, 16)
POOL_HALO = 8
D_FF = 2816
ROPE_BASE = 10000.0
N_MOD = 9
MOD_ROWS = 8
EPS = 1e-6

FF_COLS = 256
SUB_TILE = 512
FFN_SUB_TILES = 2
TOKEN_TILE = SUB_TILE * FFN_SUB_TILES
RET_TILE = 1024
RET_TILE_CONTEXT = 2048
PIECE_ROWS = 256
RET_CHUNK = 256
COL_STAGE_DEPTH = 4
ROW_STAGE_DEPTH = 2
PANEL_SPLIT = 4
V7X_VMEM_LIMIT = 56 * 1024 * 1024

F32 = jnp.float32
BF16 = jnp.bfloat16


def _silu(x):
    return x * (1.0 / (1.0 + jnp.exp(-x)))


def _rmsnorm(x, g):
    return x * lax.rsqrt(jnp.mean(x * x, axis=-1, keepdims=True) + EPS) * g


def _dot(a, b):
    return jnp.dot(a, b, preferred_element_type=F32)


def _dot_tn(a, b):
    return lax.dot_general(a, b, (((0,), (0,)), ((), ())), preferred_element_type=F32)


def _dot_nt(a, b):
    return lax.dot_general(a, b, (((1,), (1,)), ((), ())), preferred_element_type=F32)


def _head(h):
    return slice(h * HEAD_DIM, (h + 1) * HEAD_DIM)


def _resident(shape):
    nd = len(shape)
    return pl.BlockSpec(shape, lambda *_: (0,) * nd, pipeline_mode=pl.Buffered(1))


def _hbm():
    return pl.BlockSpec(memory_space=pl.ANY)


def _mods_kernel(c_ctx_ref, c_ref, w_ref, b_ref, o_ref, cond_ref):
    n_lat = c_ref.shape[0]
    cond_ref[...] = jnp.zeros(cond_ref.shape, F32)
    cond_ref[0:1, :] = c_ctx_ref[...]
    cond_ref[1:1 + n_lat, :] = c_ref[...]
    s = _silu(cond_ref[...]).astype(BF16)
    o_ref[...] = _dot(s, w_ref[...].astype(BF16)) + b_ref[...]


def _mods(c_ctx, c, ada_w, ada_b):
    n = N_MOD * D_MODEL
    assert 1 + c.shape[0] <= MOD_ROWS
    return pl.pallas_call(
        _mods_kernel,
        out_shape=jax.ShapeDtypeStruct((MOD_ROWS, n), F32),
        grid=(N_MOD,),
        in_specs=[
            pl.BlockSpec((1, D_MODEL), lambda j: (0, 0)),
            pl.BlockSpec(c.shape, lambda j: (0, 0)),
            pl.BlockSpec((D_MODEL, D_MODEL), lambda j: (0, j)),
            pl.BlockSpec((1, D_MODEL), lambda j: (0, j)),
        ],
        out_specs=pl.BlockSpec((MOD_ROWS, D_MODEL), lambda j: (0, j)),
        scratch_shapes=[pltpu.VMEM((MOD_ROWS, D_MODEL), F32)],
        name="adaln_mods",
    )(c_ctx.reshape(1, D_MODEL), c, ada_w, ada_b.reshape(1, n))


def _mod_vectors(mods_ref, mod_row0, tiles_per_mod, lead):
    row = mod_row0 + (pl.program_id(0) - lead) // tiles_per_mod
    mods = mods_ref[pl.ds(row, 1), :]
    return [mods[:, k * D_MODEL:(k + 1) * D_MODEL] for k in range(N_MOD)]


def _swiglu(a_bf16, w1_ref, w3_ref, w2_ref, act_ref):
    for c in range(D_FF // FF_COLS):
        cols = slice(c * FF_COLS, (c + 1) * FF_COLS)
        g = _dot(a_bf16, w1_ref[:, cols])
        u = _dot(a_bf16, w3_ref[:, cols])
        act_ref[:, cols] = (_silu(g) * u).astype(BF16)
    return _dot(act_ref[...], w2_ref[...])


def _panel_stream(srcs, dsts, stage_ref, sem_ref):
    depth, rows = stage_ref.shape[0], stage_ref.shape[1]
    part = rows // PANEL_SPLIT

    def copies(k):
        slot = k % depth
        return [pltpu.make_async_copy(srcs[k].at[pl.ds(s * part, part), :],
                                      stage_ref.at[slot, pl.ds(s * part, part), :], sem_ref.at[slot])
                for s in range(PANEL_SPLIT)]

    def start(k):
        for cp in copies(k):
            cp.start()

    def prime():
        for k in range(min(depth, len(srcs))):
            start(k)

    def take(k):
        for cp in copies(k):
            cp.wait()
        dsts[k][...] = stage_ref[k % depth].astype(BF16)
        if k + depth < len(srcs):
            start(k + depth)

    return prime, take


def _col_panels(ref):
    return [ref.at[:, pl.ds(p * FF_COLS, FF_COLS)] for p in range(ref.shape[1] // FF_COLS)]


def _row_panels(ref):
    return [ref.at[pl.ds(p * FF_COLS, FF_COLS), :] for p in range(ref.shape[0] // FF_COLS)]


def _load_weights(col_pairs, row_pairs, col_stage, col_sem, row_stage, row_sem):
    col_src = [p for src, _ in col_pairs for p in _col_panels(src)]
    col_dst = [p for _, dst in col_pairs for p in _col_panels(dst)]
    row_src = [p for src, _ in row_pairs for p in _row_panels(src)]
    row_dst = [p for _, dst in row_pairs for p in _row_panels(dst)]
    col_prime, col_take = _panel_stream(col_src, col_dst, col_stage, col_sem)
    row_prime, row_take = _panel_stream(row_src, row_dst, row_stage, row_sem)
    col_prime()
    row_prime()
    n_col, n_row = len(col_src), len(row_src)
    for k in range(max(n_col, n_row)):
        for kc in range(k * n_col // max(n_col, n_row), (k + 1) * n_col // max(n_col, n_row)):
            col_take(kc)
        for kr in range(k * n_row // max(n_col, n_row), (k + 1) * n_row // max(n_col, n_row)):
            row_take(kr)


def _export_copies(vmem_refs, hbm_refs, sem_ref):
    return [pltpu.make_async_copy(src, dst, sem_ref.at[k]) for k, (src, dst) in enumerate(zip(vmem_refs, hbm_refs))]


def _converting_call(compute, hbm_weights, col_major, vmem_weights, exports, stage_refs, export_sem):
    step = pl.program_id(0)
    pairs = list(zip(hbm_weights, vmem_weights))

    @pl.when(step == 0)
    def _():
        _load_weights([p for p, c in zip(pairs, col_major) if c], [p for p, c in zip(pairs, col_major) if not c],
                      *stage_refs)
        for cp in _export_copies(vmem_weights, exports, export_sem):
            cp.start()

    @pl.when(step > 0)
    def _():
        compute()

    @pl.when(step == pl.num_programs(0) - 1)
    def _():
        for cp in _export_copies(vmem_weights, exports, export_sem):
            cp.wait()


def _stage_a_compute(x_ref, mods_ref, n1_ref, nmix_ref, h1_ref, a2_ref, act_ref, w1_ref, w3_ref, w2_ref, mod_sel):
    sh1, sc1, g1, sh2, sc2 = _mod_vectors(mods_ref, *mod_sel)[0:5]
    for s in range(FFN_SUB_TILES):
        rows = slice(s * SUB_TILE, (s + 1) * SUB_TILE)
        x = x_ref[rows, :]
        a1 = _rmsnorm(x, n1_ref[...]) * (1.0 + sc1) + sh1
        h1 = x + (0.5 * g1) * _swiglu(a1.astype(BF16), w1_ref, w3_ref, w2_ref, act_ref.at[s])
        h1_ref[rows, :] = h1
        a2_ref[rows, :] = (_rmsnorm(h1, nmix_ref[...]) * (1.0 + sc2) + sh2).astype(BF16)


def _stage_a_converting_kernel(x_ref, mods_ref, n1_ref, w1_hbm, w3_hbm, w2_hbm, nmix_ref,
                               h1_ref, a2_ref, w1_out, w3_out, w2_out,
                               act_ref, w1_ref, w3_ref, w2_ref, col_stage, col_sem, row_stage, row_sem, export_sem,
                               *, mod_sel):
    compute = functools.partial(_stage_a_compute, x_ref, mods_ref, n1_ref, nmix_ref, h1_ref, a2_ref, act_ref,
                                w1_ref, w3_ref, w2_ref, mod_sel)
    _converting_call(compute, (w1_hbm, w3_hbm, w2_hbm), (True, True, False), (w1_ref, w3_ref, w2_ref),
                     (w1_out, w3_out, w2_out), (col_stage, col_sem, row_stage, row_sem), export_sem)


def _stage_a_kernel(x_ref, mods_ref, n1_ref, w1_ref, w3_ref, w2_ref, nmix_ref, h1_ref, a2_ref, act_ref, *, mod_sel):
    _stage_a_compute(x_ref, mods_ref, n1_ref, nmix_ref, h1_ref, a2_ref, act_ref, w1_ref, w3_ref, w2_ref, mod_sel)


def _stage_c_compute(h1_ref, mixin_ref, mods_ref, n2_ref, nf_ref, y_ref, act_ref,
                     wout_ref, w1_ref, w3_ref, w2_ref, mod_sel):
    g2, sh3, sc3, g3 = _mod_vectors(mods_ref, *mod_sel)[5:9]
    for s in range(FFN_SUB_TILES):
        rows = slice(s * SUB_TILE, (s + 1) * SUB_TILE)
        h2 = h1_ref[rows, :] + g2 * _dot(mixin_ref[rows, :], wout_ref[...])
        a3 = _rmsnorm(h2, n2_ref[...]) * (1.0 + sc3) + sh3
        h3 = h2 + (0.5 * g3) * _swiglu(a3.astype(BF16), w1_ref, w3_ref, w2_ref, act_ref.at[s])
        y_ref[rows, :] = _rmsnorm(h3, nf_ref[...])


def _stage_c_converting_kernel(h1_ref, mixin_ref, mods_ref, wout_hbm, n2_ref, w1_hbm, w3_hbm, w2_hbm, nf_ref,
                               y_ref, wout_out, w1_out, w3_out, w2_out,
                               act_ref, w1_ref, w3_ref, w2_ref, col_stage, col_sem, row_stage, row_sem, export_sem,
                               wout_ref, *, mod_sel):
    compute = functools.partial(_stage_c_compute, h1_ref, mixin_ref, mods_ref, n2_ref, nf_ref, y_ref, act_ref,
                                wout_ref, w1_ref, w3_ref, w2_ref, mod_sel)
    _converting_call(compute, (wout_hbm, w1_hbm, w3_hbm, w2_hbm), (True, True, True, False),
                     (wout_ref, w1_ref, w3_ref, w2_ref), (wout_out, w1_out, w3_out, w2_out),
                     (col_stage, col_sem, row_stage, row_sem), export_sem)


def _stage_c_kernel(h1_ref, mixin_ref, mods_ref, wout_ref, n2_ref, w1_ref, w3_ref, w2_ref, nf_ref,
                    y_ref, act_ref, *, mod_sel):
    _stage_c_compute(h1_ref, mixin_ref, mods_ref, n2_ref, nf_ref, y_ref, act_ref,
                     wout_ref, w1_ref, w3_ref, w2_ref, mod_sel)


def _ffn_stage_specs(weight, mod_row0, rows_per_mod):
    converting = weight.dtype == F32
    lead = 1 if converting else 0
    tile = lambda width: pl.BlockSpec((TOKEN_TILE, width), lambda i: (jnp.maximum(i - lead, 0), 0))
    weight_spec = (lambda w: _hbm()) if converting else (lambda w: _resident(w.shape))
    return tile, weight_spec, converting, lead, (mod_row0, rows_per_mod // TOKEN_TILE, lead)


def _act_scratch():
    return [pltpu.VMEM((FFN_SUB_TILES, SUB_TILE, D_FF), BF16)]


def _conversion_scratch(weights):
    return ([pltpu.VMEM(w.shape, BF16) for w in weights]
            + [pltpu.VMEM((COL_STAGE_DEPTH, D_MODEL, FF_COLS), F32), pltpu.SemaphoreType.DMA((COL_STAGE_DEPTH,)),
               pltpu.VMEM((ROW_STAGE_DEPTH, FF_COLS, D_MODEL), F32), pltpu.SemaphoreType.DMA((ROW_STAGE_DEPTH,)),
               pltpu.SemaphoreType.DMA((len(weights),))])


def _stage_a(x, mods, mod_row0, rows_per_mod, n1, w1, w3, w2, nmix, name):
    tile, weight_spec, converting, lead, mod_sel = _ffn_stage_specs(w1, mod_row0, rows_per_mod)
    tokens, d, tm = x.shape[0], D_MODEL, TOKEN_TILE
    out_shape = [jax.ShapeDtypeStruct((tokens, d), F32), jax.ShapeDtypeStruct((tokens, d), BF16)]
    out_specs = [tile(d), tile(d)]
    scratch = _act_scratch()
    if converting:
        out_shape += [jax.ShapeDtypeStruct(w.shape, BF16) for w in (w1, w3, w2)]
        out_specs += [_hbm()] * 3
        scratch += _conversion_scratch((w1, w3, w2))
    outs = pl.pallas_call(
        functools.partial(_stage_a_converting_kernel if converting else _stage_a_kernel, mod_sel=mod_sel),
        out_shape=out_shape,
        grid=(lead + tokens // tm,),
        in_specs=[tile(d), _resident(mods.shape), _resident((1, d)),
                  weight_spec(w1), weight_spec(w3), weight_spec(w2), _resident((1, d))],
        out_specs=out_specs,
        scratch_shapes=scratch,
        compiler_params=pltpu.CompilerParams(
            dimension_semantics=("arbitrary",), vmem_limit_bytes=V7X_VMEM_LIMIT),
        name=name,
    )(x, mods, n1, w1, w3, w2, nmix)
    return outs[0], outs[1], tuple(outs[2:])


def _stage_c(h1, mixin, mods, mod_row0, rows_per_mod, w_out, n2, w1, w3, w2, nf, name):
    tile, weight_spec, converting, lead, mod_sel = _ffn_stage_specs(w1, mod_row0, rows_per_mod)
    tokens, d, tm = h1.shape[0], D_MODEL, TOKEN_TILE
    out_shape = [jax.ShapeDtypeStruct((tokens, d), F32)]
    out_specs = [tile(d)]
    scratch = _act_scratch()
    if converting:
        out_shape += [jax.ShapeDtypeStruct(w.shape, BF16) for w in (w_out, w1, w3, w2)]
        out_specs += [_hbm()] * 4
        conv = _conversion_scratch((w1, w3, w2))
        conv[-1] = pltpu.SemaphoreType.DMA((4,))
        scratch += conv + [pltpu.VMEM(w_out.shape, BF16)]
    outs = pl.pallas_call(
        functools.partial(_stage_c_converting_kernel if converting else _stage_c_kernel, mod_sel=mod_sel),
        out_shape=out_shape,
        grid=(lead + tokens // tm,),
        in_specs=[tile(d), tile(MIX_WIDTH), _resident(mods.shape),
                  weight_spec(w_out), _resident((1, d)), weight_spec(w1), weight_spec(w3), weight_spec(w2),
                  _resident((1, d))],
        out_specs=out_specs,
        scratch_shapes=scratch,
        compiler_params=pltpu.CompilerParams(
            dimension_semantics=("arbitrary",), vmem_limit_bytes=V7X_VMEM_LIMIT),
        name=name,
    )(h1, mixin, mods, w_out, n2, w1, w3, w2, nf)
    return outs[0], tuple(outs[1:])


def _ret_tables(dec_f_ref, dec_b_ref, dmat_ref, qd_ref, kd_ref, cd_ref):
    c = RET_CHUNK
    row = lax.broadcasted_iota(jnp.int32, (c, c), 0)
    col = lax.broadcasted_iota(jnp.int32, (c, c), 1)
    rel = (row - col).astype(F32)
    r = row[:, :HEAD_DIM].astype(F32)
    for h in range(N_HEADS):
        lg_f = -jnp.exp(jnp.broadcast_to(dec_f_ref[0:1, h:h + 1], (1, c)))
        lg_b = -jnp.exp(jnp.broadcast_to(dec_b_ref[0:1, h:h + 1], (1, c)))
        dmat_ref[h] = (jnp.where(rel >= 0, jnp.exp(lg_f * jnp.maximum(rel, 0.0)), 0.0)
                       + jnp.where(rel <= 0, jnp.exp(lg_b * jnp.maximum(-rel, 0.0)), 0.0))
        lf, lb = lg_f[:, :HEAD_DIM], lg_b[:, :HEAD_DIM]
        qd_ref[h] = jnp.concatenate([jnp.exp(lf * (r + 1.0)), jnp.exp(lb * (c - r))], axis=1)
        kd_ref[h] = jnp.concatenate([jnp.exp(lf * (c - 1.0 - r)), jnp.exp(lb * r)], axis=1)
        cd_ref[h] = jnp.concatenate([jnp.exp(lf * c), jnp.exp(lb * c)], axis=1)


def _ret_table_scratch():
    c = RET_CHUNK
    return [pltpu.VMEM((N_HEADS, c, c), F32), pltpu.VMEM((N_HEADS, c, 2 * HEAD_DIM), F32),
            pltpu.VMEM((N_HEADS, c, 2 * HEAD_DIM), F32), pltpu.VMEM((N_HEADS, 1, 2 * HEAD_DIM), F32)]


def _project(a2, win_ref, part, rot):
    t = _dot(a2, win_ref[:, part * RET_WIDTH:(part + 1) * RET_WIDTH])
    if part == 1:
        t = t * (HEAD_DIM ** -0.5)
    if part < 2 and rot is not None:
        cos2, sin2 = rot
        t = jnp.concatenate([t[:, _head(h)] * cos2 + pltpu.roll(t[:, _head(h)], HEAD_DIM // 2, axis=1) * sin2
                             for h in range(N_HEADS)], axis=-1)
    return t


def _chunk_kv(k_c, v_c, kd):
    v32 = v_c.astype(F32)
    vd = (jnp.concatenate([v32, v32], axis=1) * kd).astype(BF16)
    return _dot_tn(k_c, vd)


def _pool_inv_count(t0, rows, seq_len, window):
    def edge(start):
        t = start + lax.broadcasted_iota(jnp.int32, (POOL_HALO, HEAD_DIM), 0)
        cnt = jnp.minimum(t + window // 2, seq_len) - jnp.maximum(t - window // 2, 0)
        return 1.0 / cnt.astype(F32)
    inner = jnp.full((rows - 2 * POOL_HALO, HEAD_DIM), 1.0 / window, F32)
    return jnp.concatenate([edge(t0), inner, edge(t0 + rows - POOL_HALO)], axis=0)


def _pool_centred(ext, t0, rows, seq_len, window):
    n_ext = rows + 2 * POOL_HALO
    half = window // 2
    acc, span = ext, 1
    while span < half:
        acc = acc + pltpu.roll(acc, n_ext - span, axis=0)
        span *= 2
    acc = acc + pltpu.roll(acc, half, axis=0)
    win = acc[POOL_HALO:POOL_HALO + rows]
    tok = ext[POOL_HALO:POOL_HALO + rows]
    return win * _pool_inv_count(t0, rows, seq_len, window) - tok


def _pool_groups(ext_of_group, pw_ref, ps_ref, t0, rows, seq_len, store):
    for g, window in enumerate(POOL_WINDOWS):
        centred = _pool_centred(ext_of_group(g), t0, rows, seq_len, window)
        store(g, (_dot(centred.astype(BF16), pw_ref[g].astype(BF16)) * ps_ref[:, _head(g)]).astype(BF16))


def _software_pipeline(piece_starts, projection, mixing):
    prev = None
    for r0 in piece_starts:
        out, parts = projection(r0)
        stages = mixing(*prev) if prev is not None else []
        for i, run_mxu in enumerate(parts):
            run_mxu()
            for run_vpu in stages[i * len(stages) // len(parts):(i + 1) * len(stages) // len(parts)]:
                run_vpu()
        prev = (out, r0)
    for run in mixing(*prev):
        run()


def _stage_b_context_kernel(a2_ref, win_ref, dec_f_ref, dec_b_ref, gn_ref, pw_ref, ps_ref,
                            mix_ref, sf_ref, sb_ref, dmat_ref, qd_ref, kd_ref, cd_ref, ext_ref,
                            *, seq_len):
    @pl.when(pl.program_id(0) == 0)
    def _():
        _ret_tables(dec_f_ref, dec_b_ref, dmat_ref, qd_ref, kd_ref, cd_ref)
        zeros = jnp.zeros((POOL_HALO, RET_WIDTH), F32)
        for s in range(ext_ref.shape[0]):
            ext_ref[s, 0:POOL_HALO, :] = zeros
            ext_ref[s, POOL_HALO + seq_len:, :] = zeros

    piece = PIECE_ROWS
    seqs_per_piece = piece // seq_len

    def projection(r0):
        a2 = a2_ref[r0:r0 + piece, :]
        out = {}

        def part(p):
            def run():
                t = _project(a2, win_ref, p, None)
                out[p] = t.astype(BF16) if p < 3 else t
            return run
        return out, [part(p) for p in range(5)]

    def mixing(out, r0):
        units = [(s, h, slice(s * seq_len, (s + 1) * seq_len)) for s in range(seqs_per_piece) for h in range(N_HEADS)]
        tile_rows = lambda rows: slice(r0 + rows.start, r0 + rows.stop)
        st = {}

        def scores():
            st["p"] = [(_dot_nt(out[0][rows, _head(h)], out[1][rows, _head(h)]) * dmat_ref[h]).astype(BF16)
                       for s, h, rows in units]

        def values():
            st["o"] = [_dot(p, out[2][rows, _head(h)]) for p, (s, h, rows) in zip(st["p"], units)]

        def outputs():
            for o, (s, h, rows) in zip(st["o"], units):
                o = o * lax.rsqrt(jnp.mean(o * o, axis=-1, keepdims=True) + EPS) * gn_ref[:, _head(h)]
                mix_ref[tile_rows(rows), _head(h)] = (o * _silu(out[3][rows, _head(h)])).astype(BF16)

        def states():
            for s, h, rows in units:
                kv = _chunk_kv(out[1][rows, _head(h)], out[2][rows, _head(h)], kd_ref[h])
                seq = r0 // seq_len + s
                sf_ref[seq, 0, h] = kv[:, :HEAD_DIM]
                sb_ref[seq, 0, h] = kv[:, HEAD_DIM:]

        def pooling():
            for s in range(seqs_per_piece):
                rows = slice(s * seq_len, (s + 1) * seq_len)
                seq = r0 // seq_len + s
                ext_ref[seq, POOL_HALO:POOL_HALO + seq_len, :] = out[4][rows, :]

                def store(g, val, rows=rows):
                    mix_ref[tile_rows(rows), _head(N_HEADS + g)] = val
                _pool_groups(lambda g, seq=seq: ext_ref[seq, :, _head(g)], pw_ref, ps_ref, 0, seq_len, seq_len, store)
        return [scores, values, outputs, states, pooling]

    _software_pipeline(range(0, a2_ref.shape[0], piece), projection, mixing)


def _stage_b_context(a2, w_in, dec_f, dec_b, gn, pool_w, pool_scale, n_seq, seq_len):
    assert seq_len == RET_CHUNK
    tokens = n_seq * seq_len
    tl = RET_TILE_CONTEXT
    seqs = tl // seq_len
    tile = lambda width: pl.BlockSpec((tl, width), lambda i: (i, 0))
    state_shape = jax.ShapeDtypeStruct((n_seq, 1, N_HEADS, HEAD_DIM, HEAD_DIM), F32)
    state_spec = lambda: pl.BlockSpec((seqs, 1, N_HEADS, HEAD_DIM, HEAD_DIM), lambda i: (i, 0, 0, 0, 0))
    return pl.pallas_call(
        functools.partial(_stage_b_context_kernel, seq_len=seq_len),
        out_shape=[jax.ShapeDtypeStruct((tokens, MIX_WIDTH), BF16)] + [state_shape] * 2,
        grid=(tokens // tl,),
        in_specs=[tile(D_MODEL), _resident(w_in.shape),
                  _resident(dec_f.shape), _resident(dec_b.shape), _resident(gn.shape),
                  _resident(pool_w.shape), _resident(pool_scale.shape)],
        out_specs=[tile(MIX_WIDTH), state_spec(), state_spec()],
        scratch_shapes=_ret_table_scratch() + [pltpu.VMEM((seqs, seq_len + 2 * POOL_HALO, RET_WIDTH), F32)],
        compiler_params=pltpu.CompilerParams(
            dimension_semantics=("arbitrary",), vmem_limit_bytes=V7X_VMEM_LIMIT),
        name="stage_b_context",
    )(a2, w_in, dec_f, dec_b, gn, pool_w, pool_scale)


def _stage_b_latent_kernel(a2_ref, cos_ref, sin_ref, win_ref, dec_f_ref, dec_b_ref, gn_ref, pw_ref, ps_ref,
                           s0f_ref, s0b_ref,
                           mix_ref,
                           dmat_ref, qd_ref, kd_ref, cd_ref, sf_ref, sb_ref, sb_hist_ref, kvf_hist_ref,
                           kseq_ref, vseq_ref, useq_ref, sf_tile_ref,
                           *, seq_len, n_tiles):
    c = RET_CHUNK
    tl = a2_ref.shape[1]
    chunks = tl // c
    j = pl.program_id(1)
    rot = (cos_ref[...], sin_ref[...])

    @pl.when((pl.program_id(0) == 0) & (j == 0))
    def _():
        _ret_tables(dec_f_ref, dec_b_ref, dmat_ref, qd_ref, kd_ref, cd_ref)
        zeros = jnp.zeros((POOL_HALO, RET_WIDTH), F32)
        useq_ref[0:POOL_HALO, :] = zeros
        useq_ref[POOL_HALO + seq_len:, :] = zeros

    @pl.when(j == 0)
    def _():
        for h in range(N_HEADS):
            sb_ref[h] = s0b_ref[0, h]
            sf_ref[h] = s0f_ref[0, h]

    @pl.when(j < n_tiles)
    def _():
        tile = n_tiles - 1 - j
        t0 = pl.multiple_of(tile * tl, tl)
        piece = PIECE_ROWS

        def projection(r0):
            a2 = a2_ref[0, r0:r0 + piece, :]
            rot_rows = (cos_ref[r0:r0 + piece, :], sin_ref[r0:r0 + piece, :])
            out = {}

            def part(p, seq_ref, offset):
                def run():
                    t = _project(a2, win_ref, p, rot_rows)
                    out[p] = t.astype(seq_ref.dtype)
                    seq_ref[pl.ds(t0 + r0 + offset, piece), :] = out[p]
                return run
            return out, [part(1, kseq_ref, 0), part(2, vseq_ref, 0), part(4, useq_ref, POOL_HALO)]

        def mixing(out, r0):
            def states():
                for ci in reversed(range(piece // c)):
                    n = tile * chunks + r0 // c + ci
                    rows = slice(ci * c, (ci + 1) * c)
                    for h in range(N_HEADS):
                        kv = _chunk_kv(out[1][rows, _head(h)], out[2][rows, _head(h)], kd_ref[h])
                        s_b = sb_ref[h]
                        sb_hist_ref[n, h] = s_b.astype(BF16)
                        kvf_hist_ref[n, h] = kv[:, :HEAD_DIM]
                        sb_ref[h] = s_b * cd_ref[h][:, HEAD_DIM:] + kv[:, HEAD_DIM:]
            return [states]

        _software_pipeline(reversed(range(0, tl, piece)), projection, mixing)

    @pl.when(j >= n_tiles)
    def _():
        tile = j - n_tiles
        t0 = pl.multiple_of(tile * tl, tl)
        piece = PIECE_ROWS

        for h in range(N_HEADS):
            s_f = sf_ref[h]
            for ci in range(chunks):
                sf_tile_ref[ci, h] = s_f.astype(BF16)
                s_f = s_f * cd_ref[h][:, :HEAD_DIM] + kvf_hist_ref[tile * chunks + ci, h]
            sf_ref[h] = s_f

        def projection(r0):
            a2 = a2_ref[0, r0:r0 + piece, :]
            rot_rows = (cos_ref[r0:r0 + piece, :], sin_ref[r0:r0 + piece, :])
            out = {}

            def part(p):
                def run():
                    t = _project(a2, win_ref, p, rot_rows)
                    out[p] = t.astype(BF16) if p == 0 else t
                return run
            return out, [part(0), part(3)]

        def mixing(out, r0):
            units = [(r0 // c + ci, h, slice(ci * c, (ci + 1) * c)) for ci in range(piece // c) for h in range(N_HEADS)]
            seq_rows = lambda rows: pl.ds(t0 + r0 + rows.start, c)
            st = {}

            def scores():
                st["p"] = [(_dot_nt(out[0][rows, _head(h)], kseq_ref[seq_rows(rows), _head(h)])
                            * dmat_ref[h]).astype(BF16) for ct, h, rows in units]

            def values():
                st["o"] = []
                for p, (ct, h, rows) in zip(st["p"], units):
                    s2 = jnp.concatenate([sf_tile_ref[ct, h], sb_hist_ref[tile * chunks + ct, h]], axis=1)
                    cross = _dot(out[0][rows, _head(h)], s2) * qd_ref[h]
                    st["o"].append(_dot(p, vseq_ref[seq_rows(rows), _head(h)])
                                   + cross[:, :HEAD_DIM] + cross[:, HEAD_DIM:])

            def outputs():
                for o, (ct, h, rows) in zip(st["o"], units):
                    o = o * lax.rsqrt(jnp.mean(o * o, axis=-1, keepdims=True) + EPS) * gn_ref[:, _head(h)]
                    mix_ref[0, r0 + rows.start:r0 + rows.stop, _head(h)] = (
                        o * _silu(out[3][rows, _head(h)])).astype(BF16)

            def pooling():
                def store(g, val):
                    mix_ref[0, r0:r0 + piece, _head(N_HEADS + g)] = val
                _pool_groups(lambda g: useq_ref[pl.ds(t0 + r0, piece + 2 * POOL_HALO), _head(g)], pw_ref, ps_ref,
                             tile * tl + r0, piece, seq_len, store)
            return [scores, values, outputs, pooling]

        _software_pipeline(range(0, tl, piece), projection, mixing)


def _stage_b_latent(a2, rot, w_in, dec_f, dec_b, gn, pool_w, pool_scale, s0f, s0b):
    b, l, d = a2.shape
    tl = RET_TILE
    nt = l // tl
    n_chunks = l // RET_CHUNK
    both_tile = lambda j: jnp.where(j < nt, nt - 1 - j, j - nt)
    fwd = lambda: pl.BlockSpec((1, tl, MIX_WIDTH), lambda i, j: (i, jnp.maximum(j - nt, 0), 0))
    rot_spec = lambda: pl.BlockSpec((tl, HEAD_DIM), lambda i, j: (both_tile(j), 0))
    state = lambda: pl.BlockSpec((1, N_HEADS, HEAD_DIM, HEAD_DIM), lambda i, j: (i, 0, 0, 0))
    return pl.pallas_call(
        functools.partial(_stage_b_latent_kernel, seq_len=l, n_tiles=nt),
        out_shape=jax.ShapeDtypeStruct((b, l, MIX_WIDTH), BF16),
        grid=(b, 2 * nt),
        in_specs=[pl.BlockSpec((1, tl, d), lambda i, j: (i, both_tile(j), 0)), rot_spec(), rot_spec(),
                  _resident(w_in.shape), _resident(dec_f.shape), _resident(dec_b.shape), _resident(gn.shape),
                  _resident(pool_w.shape), _resident(pool_scale.shape), state(), state()],
        out_specs=fwd(),
        scratch_shapes=_ret_table_scratch() + [
            pltpu.VMEM((N_HEADS, HEAD_DIM, HEAD_DIM), F32), pltpu.VMEM((N_HEADS, HEAD_DIM, HEAD_DIM), F32),
            pltpu.VMEM((n_chunks, N_HEADS, HEAD_DIM, HEAD_DIM), BF16),
            pltpu.VMEM((n_chunks, N_HEADS, HEAD_DIM, HEAD_DIM), F32),
            pltpu.VMEM((l, RET_WIDTH), BF16), pltpu.VMEM((l, RET_WIDTH), BF16),
            pltpu.VMEM((l + 2 * POOL_HALO, RET_WIDTH), F32),
            pltpu.VMEM((tl // RET_CHUNK, N_HEADS, HEAD_DIM, HEAD_DIM), BF16)],
        compiler_params=pltpu.CompilerParams(
            dimension_semantics=("arbitrary", "arbitrary"), vmem_limit_bytes=V7X_VMEM_LIMIT),
        name="stage_b_latent",
    )(a2, *rot, w_in, dec_f, dec_b, gn, pool_w, pool_scale, s0f, s0b)


def _rotary_tables(seq_len):
    rows = seq_len // GRID_W
    row = np.repeat(np.arange(rows, dtype=np.float64), GRID_W)
    col = np.tile(np.arange(GRID_W, dtype=np.float64), rows)
    n_half = HEAD_DIM // 4
    freqs = ROPE_BASE ** (-np.arange(n_half, dtype=np.float64) / n_half)
    ang = np.concatenate([row[:, None] * freqs, col[:, None] * freqs], axis=-1)
    cos, sin = np.cos(ang), np.sin(ang)
    cos2 = np.concatenate([cos, cos], axis=-1).astype(np.float32)
    sin2 = np.concatenate([-sin, sin], axis=-1).astype(np.float32)
    return jnp.asarray(cos2), jnp.asarray(sin2)


def kernel(x_prompt, x_sample, state_ret_fwd, state_ret_bwd, c, c_ctx, ada_w, ada_b, norm_ffn1, ffn1_w1, ffn1_w3, ffn1_w2, norm_mix, w_in, ret_decay_fwd, ret_decay_bwd, ret_gn, pool_w, pool_scale, w_out, norm_ffn2, ffn2_w1, ffn2_w3, ffn2_w2, norm_final):
    depth = ada_w.shape[0]
    assert depth == 1, "single trunk layer"
    d = D_MODEL
    n_ctx, l_ctx, _ = x_prompt.shape
    n_lat, l_lat, _ = x_sample.shape
    t_ctx, t_lat = n_ctx * l_ctx, n_lat * l_lat

    mods = _mods(c_ctx, c, ada_w[0], ada_b[0])

    row = lambda g: g.reshape(1, -1)
    w_in_b = w_in[0].astype(BF16)
    n1, nmix, n2, nf = row(norm_ffn1[0]), row(norm_mix[0]), row(norm_ffn2[0]), row(norm_final)
    dec_f, dec_b = ret_decay_fwd[0].reshape(1, N_HEADS), ret_decay_bwd[0].reshape(1, N_HEADS)
    gn, ps = row(ret_gn[0]), row(pool_scale[0])

    h1c, a2c, ffn1_b = _stage_a(x_prompt.reshape(t_ctx, d), mods, 0, t_ctx, n1, ffn1_w1[0], ffn1_w3[0], ffn1_w2[0],
                                nmix, "stage_a_context")
    mixc, new_f, new_b = _stage_b_context(a2c, w_in_b, dec_f, dec_b, gn, pool_w[0], ps, n_ctx, l_ctx)
    y_prompt, (w_out_b, *ffn2_b) = _stage_c(h1c, mixc, mods, 0, t_ctx, w_out[0], n2,
                                           ffn2_w1[0], ffn2_w3[0], ffn2_w2[0], nf, "stage_c_context")

    h1l, a2l, _ = _stage_a(x_sample.reshape(t_lat, d), mods, 1, l_lat, n1, *ffn1_b, nmix, "stage_a_latent")
    mixl = _stage_b_latent(a2l.reshape(n_lat, l_lat, d), _rotary_tables(l_lat), w_in_b, dec_f, dec_b, gn,
                           pool_w[0], ps, state_ret_fwd[:, 0], state_ret_bwd[:, 0])
    y_sample, _ = _stage_c(h1l, mixl.reshape(t_lat, MIX_WIDTH), mods, 1, l_lat,
                           w_out_b, n2, *ffn2_b, nf, "stage_c_latent")

    return (y_prompt.reshape(n_ctx, l_ctx, d), y_sample.reshape(n_lat, l_lat, d), new_f, new_b)
```
